```python
import math
import jax, jax.numpy as jnp
from jax import lax
import numpy as np

D_MODEL = 1024
BATCH = 4
SEQ = 4096
DEPTH = 4
DEC_BATCH = 32
DEC_SEQ = 4
PAST_LEN = 8192
PAGE_SIZE = 128

N_ATTN_LAYERS = (DEPTH + 1) // 2
N_CONV_LAYERS = DEPTH // 2
A_WIDTH = D_MODEL // 2
HD_A = 64
H_A = A_WIDTH // (2 * HD_A)
DK_A = 2 * HD_A
DV_A = 2 * HD_A
B_WIDTH = D_MODEL - A_WIDTH
CONV_B_WIDTH = 31
C_WIDTH = D_MODEL
CONV_C_WIDTH = 3
IN_EVEN = 3 * A_WIDTH + 2 * B_WIDTH
N_GROUPS = 4
EXPERTS_PER_GROUP = 8
N_EXPERTS = N_GROUPS * EXPERTS_PER_GROUP
TOP_K_IN_GROUP = 2
F_EXPERT = 512
MOE_BLOCK = 128
Q_BLOCK = 128
ALIBI_MAX = 8.0
EPS = 1e-6
NEG_BIG = -1e30

kernel_name = 'hybrid_diffattn_conformer_shortconv_hmoe_step'


def rmsnorm(x, g):
    x32 = x.astype(jnp.float32)
    y = x32 * lax.rsqrt(jnp.mean(x32 * x32, axis=-1, keepdims=True) + EPS)
    return (y * g.astype(jnp.float32)).astype(x.dtype)


def layernorm(x, g, b):
    x32 = x.astype(jnp.float32)
    mu = jnp.mean(x32, axis=-1, keepdims=True)
    xc = x32 - mu
    y = xc * lax.rsqrt(jnp.mean(xc * xc, axis=-1, keepdims=True) + EPS)
    return (y * g.astype(jnp.float32) + b.astype(jnp.float32)).astype(x.dtype)


def causal_dwconv(u, buf, w):
    xp = jnp.concatenate([buf.astype(u.dtype), u], axis=1)
    y = lax.conv_general_dilated(xp, w.astype(u.dtype), window_strides=(1,), padding='VALID',
                                 dimension_numbers=('NWC', 'WIO', 'NWC'), feature_group_count=u.shape[-1])
    new_buf = xp[:, xp.shape[1] - (w.shape[0] - 1):]
    return y, new_buf


def ada_mod(c, w, b):
    return jnp.split(jax.nn.silu(c) @ w + b, 6, axis=-1)


def modulate(x, g, shift, scale):
    return rmsnorm(x, g) * (1 + scale[:, None, :]) + shift[:, None, :]


def diff_attn_core(q, k, v, q_pos, k_pos, lam):
    slopes = jnp.exp2(-ALIBI_MAX * jnp.arange(1, H_A + 1, dtype=jnp.float32) / H_A)
    dist = (q_pos[:, None] - k_pos[None, :]).astype(jnp.float32)
    bias = -slopes[:, None, None] * jnp.abs(dist)
    causal = dist >= 0
    scale = HD_A ** -0.5

    def attn_map(qa, ka):
        s = jnp.einsum('bqhd,bkhd->bhqk', qa, ka).astype(jnp.float32) * scale + bias
        return jax.nn.softmax(jnp.where(causal, s, NEG_BIG), axis=-1)

    p = attn_map(q[..., :HD_A], k[..., :HD_A]) - lam * attn_map(q[..., HD_A:], k[..., HD_A:])
    return jnp.einsum('bhqk,bkhd->bqhd', p.astype(v.dtype), v)


def even_mixer(h, k_past, v_past, conv_buf, w_in, q_gain, k_gain, lam_q1, lam_k1, lam_q2, lam_k2,
               lam_init, sub_gain, cb_w, cb_b, cb_ln_g, cb_ln_b, w_out):
    bn, t, _ = h.shape
    proj = h @ w_in
    q, k, v, ga, gb = jnp.split(proj, [A_WIDTH, 2 * A_WIDTH, 3 * A_WIDTH, 3 * A_WIDTH + B_WIDTH], axis=-1)
    q = rmsnorm(q.reshape(bn, t, H_A, 2, HD_A), q_gain).reshape(bn, t, H_A, DK_A)
    k = rmsnorm(k.reshape(bn, t, H_A, 2, HD_A), k_gain).reshape(bn, t, H_A, DK_A)
    v = v.reshape(bn, t, H_A, DV_A)
    lam = (jnp.exp(jnp.sum((lam_q1 * lam_k1).astype(jnp.float32)))
           - jnp.exp(jnp.sum((lam_q2 * lam_k2).astype(jnp.float32))) + lam_init)
    if k_past is None:
        pos = jnp.arange(t, dtype=jnp.int32)
        nb = t // Q_BLOCK
        qb = jnp.moveaxis(q.reshape(bn, nb, Q_BLOCK, H_A, DK_A), 1, 0)
        pb = pos.reshape(nb, Q_BLOCK)
        ob = lax.map(lambda a: diff_attn_core(a[0], k, v, a[1], pos, lam), (qb, pb))
        o = jnp.moveaxis(ob, 0, 1).reshape(bn, t, H_A, DV_A)
    else:
        past = k_past.shape[1]
        k_all = jnp.concatenate([k_past.astype(k.dtype), k], axis=1)
        v_all = jnp.concatenate([v_past.astype(v.dtype), v], axis=1)
        k_pos = jnp.arange(past + t, dtype=jnp.int32)
        q_pos = past + jnp.arange(t, dtype=jnp.int32)
        o = diff_attn_core(q, k_all, v_all, q_pos, k_pos, lam)
    o = (rmsnorm(o, sub_gain) * (1.0 - lam_init)).reshape(bn, t, A_WIDTH)
    u = ga * jax.nn.sigmoid(gb)
    cv, new_buf = causal_dwconv(u, conv_buf, cb_w)
    cv = jax.nn.silu(layernorm(cv + cb_b, cb_ln_g, cb_ln_b))
    out = jnp.concatenate([o, cv], axis=-1) @ w_out
    return out, k, v, new_buf


def odd_mixer(h, conv_buf, w_in, conv_w, w_out):
    bg, cg, hx = jnp.split(h @ w_in, 3, axis=-1)
    y, new_buf = causal_dwconv(cg * hx, conv_buf, conv_w)
    return (bg * y) @ w_out, new_buf


def hier_moe(x, w_group, b_group, w_router, b_router, w_gate_up, w_down):
    bn, t, d = x.shape
    xf = x.reshape(-1, d)
    n = xf.shape[0]
    g_logits = (xf @ w_group).astype(jnp.float32) + b_group
    g_idx = jnp.argmax(g_logits, axis=-1)
    g_w = jnp.take_along_axis(jax.nn.softmax(g_logits, axis=-1), g_idx[:, None], axis=-1)
    e_logits = ((xf @ w_router).astype(jnp.float32) + b_router).reshape(n, N_GROUPS, EXPERTS_PER_GROUP)
    e_in_group = jnp.take_along_axis(e_logits, g_idx[:, None, None], axis=1)[:, 0]
    top_v, top_i = lax.top_k(e_in_group, TOP_K_IN_GROUP)
    weights = g_w * jax.nn.softmax(top_v, axis=-1)
    expert = (g_idx[:, None] * EXPERTS_PER_GROUP + top_i).reshape(-1).astype(jnp.int32)
    nk = n * TOP_K_IN_GROUP
    order = jnp.argsort(expert)
    e_sorted = expert[order]
    tok_sorted = order // TOP_K_IN_GROUP
    w_sorted = weights.reshape(-1)[order]
    sizes = jnp.bincount(expert, length=N_EXPERTS).astype(jnp.int32)
    padded = ((sizes + MOE_BLOCK - 1) // MOE_BLOCK) * MOE_BLOCK
    pad_end = jnp.cumsum(padded)
    pad_start = pad_end - padded
    start = jnp.cumsum(sizes) - sizes
    dest = pad_start[e_sorted] + jnp.arange(nk, dtype=jnp.int32) - start[e_sorted]
    n_blocks = (nk + MOE_BLOCK - 1) // MOE_BLOCK + N_EXPERTS
    xp = jnp.zeros((n_blocks * MOE_BLOCK, d), x.dtype).at[dest].set(xf[tok_sorted])
    block_e = jnp.minimum(jnp.searchsorted(pad_end, jnp.arange(n_blocks, dtype=jnp.int32) * MOE_BLOCK, side='right'),
                          N_EXPERTS - 1)

    def expert_block(a):
        xb, e = a
        gate, up = jnp.split(xb @ w_gate_up[e], 2, axis=-1)
        return (jax.nn.silu(gate) * up) @ w_down[e]

    yp = lax.map(expert_block, (xp.reshape(n_blocks, MOE_BLOCK, d), block_e)).reshape(-1, d)
    y = jax.ops.segment_sum(yp[dest] * w_sorted[:, None].astype(x.dtype), tok_sorted, num_segments=n)
    return y.reshape(bn, t, d)


def setup_inputs(seed: int = 0) -> dict:
    key = jax.random.key(seed)
    ks = iter(jax.random.split(key, 40))
    f32 = jnp.float32

    def nrm(shape, scale):
        return jax.random.normal(next(ks), shape, f32) * scale

    def gain(shape):
        return 1.0 + nrm(shape, 0.05)

    n_pages = PAST_LEN // PAGE_SIZE
    n_pool = (DEC_BATCH * n_pages * 5) // 4
    d = D_MODEL
    inp = {}
    inp['x_prompt'] = nrm((BATCH, SEQ, d), 1.0)
    inp['x_sample'] = nrm((DEC_BATCH, DEC_SEQ, d), 1.0)
    inp['cache_k'] = nrm((N_ATTN_LAYERS, n_pool, PAGE_SIZE, H_A, DK_A), 1.0)
    inp['cache_v'] = nrm((N_ATTN_LAYERS, n_pool, PAGE_SIZE, H_A, DV_A), 1.0)
    inp['state_conv_b'] = nrm((N_ATTN_LAYERS, DEC_BATCH, CONV_B_WIDTH - 1, B_WIDTH), 0.5)
    inp['state_conv_c'] = nrm((N_CONV_LAYERS, DEC_BATCH, CONV_C_WIDTH - 1, C_WIDTH), 0.5)
    inp['page_table'] = jax.random.permutation(next(ks), n_pool)[: DEC_BATCH * n_pages].reshape(
        DEC_BATCH, n_pages).astype(jnp.int32)
    inp['c_prompt'] = nrm((BATCH, d), 1.0)
    inp['c_sample'] = nrm((DEC_BATCH, d), 1.0)
    inp['norm_mix_g'] = gain((DEPTH, d))
    inp['norm_ffn_g'] = gain((DEPTH, d))
    inp['w_ada'] = nrm((DEPTH, d, 6 * d), 0.02)
    inp['b_ada'] = nrm((DEPTH, 6 * d), 0.01)
    inp['w_in_a'] = nrm((N_ATTN_LAYERS, d, IN_EVEN), d ** -0.5)
    inp['q_norm_g'] = gain((N_ATTN_LAYERS, 2, HD_A))
    inp['k_norm_g'] = gain((N_ATTN_LAYERS, 2, HD_A))
    inp['lam_q1'] = nrm((N_ATTN_LAYERS, HD_A), 0.1)
    inp['lam_k1'] = nrm((N_ATTN_LAYERS, HD_A), 0.1)
    inp['lam_q2'] = nrm((N_ATTN_LAYERS, HD_A), 0.1)
    inp['lam_k2'] = nrm((N_ATTN_LAYERS, HD_A), 0.1)
    inp['subln_g'] = gain((N_ATTN_LAYERS, DV_A))
    inp['convb_w'] = nrm((N_ATTN_LAYERS, CONV_B_WIDTH, 1, B_WIDTH), CONV_B_WIDTH ** -0.5)
    inp['convb_b'] = nrm((N_ATTN_LAYERS, B_WIDTH), 0.01)
    inp['convb_ln_g'] = gain((N_ATTN_LAYERS, B_WIDTH))
    inp['convb_ln_b'] = nrm((N_ATTN_LAYERS, B_WIDTH), 0.01)
    inp['w_out_a'] = nrm((N_ATTN_LAYERS, A_WIDTH + B_WIDTH, d), (A_WIDTH + B_WIDTH) ** -0.5)
    inp['w_in_c'] = nrm((N_CONV_LAYERS, d, 3 * C_WIDTH), d ** -0.5)
    inp['convc_w'] = nrm((N_CONV_LAYERS, CONV_C_WIDTH, 1, C_WIDTH), CONV_C_WIDTH ** -0.5)
    inp['w_out_c'] = nrm((N_CONV_LAYERS, C_WIDTH, d), C_WIDTH ** -0.5)
    inp['w_group'] = nrm((DEPTH, d, N_GROUPS), d ** -0.5)
    inp['b_group'] = nrm((DEPTH, N_GROUPS), 0.01)
    inp['w_router'] = nrm((DEPTH, d, N_EXPERTS), d ** -0.5)
    inp['b_router'] = nrm((DEPTH, N_EXPERTS), 0.01)
    inp['w_gate_up'] = nrm((DEPTH, N_EXPERTS, d, 2 * F_EXPERT), d ** -0.5)
    inp['w_down'] = nrm((DEPTH, N_EXPERTS, F_EXPERT, d), F_EXPERT ** -0.5)
    return inp


def reference(x_prompt, x_sample, cache_k, cache_v, state_conv_b, state_conv_c, page_table, c_prompt, c_sample,
              norm_mix_g, norm_ffn_g, w_ada, b_ada, w_in_a, q_norm_g, k_norm_g, lam_q1, lam_k1, lam_q2, lam_k2,
              subln_g, convb_w, convb_b, convb_ln_g, convb_ln_b, w_out_a, w_in_c, convc_w, w_out_c,
              w_group, b_group, w_router, b_router, w_gate_up, w_down):
    xp, xs = x_prompt, x_sample
    bp, bs = xp.shape[0], xs.shape[0]
    k_p, v_p, k_s, v_s, cb_p, cb_s, cc_p, cc_s = [], [], [], [], [], [], [], []
    for l in range(DEPTH):
        sh1p, sc1p, g1p, sh2p, sc2p, g2p = ada_mod(c_prompt, w_ada[l], b_ada[l])
        sh1s, sc1s, g1s, sh2s, sc2s, g2s = ada_mod(c_sample, w_ada[l], b_ada[l])
        hp = modulate(xp, norm_mix_g[l], sh1p, sc1p)
        hs = modulate(xs, norm_mix_g[l], sh1s, sc1s)
        i = l // 2
        if l % 2 == 0:
            lam_init = 0.8 - 0.6 * math.exp(-0.3 * l)
            prm = (w_in_a[i], q_norm_g[i], k_norm_g[i], lam_q1[i], lam_k1[i], lam_q2[i], lam_k2[i], lam_init,
                   subln_g[i], convb_w[i], convb_b[i], convb_ln_g[i], convb_ln_b[i], w_out_a[i])
            zero_buf = jnp.zeros((bp, CONV_B_WIDTH - 1, B_WIDTH), xp.dtype)
            op, kp_new, vp_new, bufp = even_mixer(hp, None, None, zero_buf, *prm)
            k_past = cache_k[i][page_table].reshape(bs, -1, H_A, DK_A)
            v_past = cache_v[i][page_table].reshape(bs, -1, H_A, DV_A)
            os_, ks_new, vs_new, bufs = even_mixer(hs, k_past, v_past, state_conv_b[i], *prm)
            k_p.append(kp_new)
            v_p.append(vp_new)
            k_s.append(ks_new)
            v_s.append(vs_new)
            cb_p.append(bufp)
            cb_s.append(bufs)
        else:
            zero_buf = jnp.zeros((bp, CONV_C_WIDTH - 1, C_WIDTH), xp.dtype)
            op, bufp = odd_mixer(hp, zero_buf, w_in_c[i], convc_w[i], w_out_c[i])
            os_, bufs = odd_mixer(hs, state_conv_c[i], w_in_c[i], convc_w[i], w_out_c[i])
            cc_p.append(bufp)
            cc_s.append(bufs)
        xp = xp + g1p[:, None, :] * op
        xs = xs + g1s[:, None, :] * os_
        hp = modulate(xp, norm_ffn_g[l], sh2p, sc2p)
        hs = modulate(xs, norm_ffn_g[l], sh2s, sc2s)
        moe_prm = (w_group[l], b_group[l], w_router[l], b_router[l], w_gate_up[l], w_down[l])
        xp = xp + g2p[:, None, :] * hier_moe(hp, *moe_prm)
        xs = xs + g2s[:, None, :] * hier_moe(hs, *moe_prm)
    return (xp, xs, jnp.stack(k_p), jnp.stack(v_p), jnp.stack(k_s), jnp.stack(v_s),
            jnp.stack(cb_p), jnp.stack(cb_s), jnp.stack(cc_p), jnp.stack(cc_s))
```

```python
import functools
import math

import jax
import jax.numpy as jnp
from jax import lax
from jax.experimental import pallas as pl
from jax.experimental.pallas import tpu as pltpu

F32 = jnp.float32
BF16 = jnp.bfloat16

HD_A = 64
DK_A = 2 * HD_A
N_GROUPS = 4
EXPERTS_PER_GROUP = 8
N_EXPERTS = N_GROUPS * EXPERTS_PER_GROUP
TOP_K = 2
CONV_B_WIDTH = 31
CONV_C_WIDTH = 3
ALIBI_MAX = 8.0
EPS = 1e-6
NEG_BIG = -1e30

LANES = 128
SUBLANES = 8
MOE_ROWS = 256
VMEM_LIMIT = 56 * 1024 * 1024


def _cparams(sem, vmem=VMEM_LIMIT):
    return pltpu.CompilerParams(dimension_semantics=sem, vmem_limit_bytes=vmem)


def _largest_tile(n, cap, mult=SUBLANES):
    best = None
    for t in range(mult, min(n, cap) + 1, mult):
        if n % t == 0:
            best = t
    assert best is not None, (n, cap, mult)
    return best


def _modnorm(x, g, scale, shift):
    ms = jnp.mean(x * x, axis=-1, keepdims=True)
    return (x * lax.rsqrt(ms + EPS)) * g * (1.0 + scale) + shift


def _dot(a, b):
    return jnp.dot(a, b, preferred_element_type=F32)


def _dot_nt(a, b):
    return lax.dot_general(a, b, (((1,), (1,)), ((), ())), preferred_element_type=F32)


def _split3(a):
    a0 = a.astype(BF16)
    r1 = a - a0.astype(F32)
    a1 = r1.astype(BF16)
    a2 = (r1 - a1.astype(F32)).astype(BF16)
    return a0, a1, a2


def _ada_kernel(c_ref, w_ref, b_ref, o_ref):
    c = c_ref[...]
    a = (c * jax.nn.sigmoid(c)).astype(BF16)
    o_ref[...] = _dot(a, w_ref[...].astype(BF16)) + b_ref[...]


def _ada_all(c_pad, w_ada, b_ada):
    depth, d, n6 = w_ada.shape
    rows = c_pad.shape[0]
    tn = 1536
    return pl.pallas_call(
        _ada_kernel,
        out_shape=jax.ShapeDtypeStruct((depth, rows, n6), F32),
        grid=(depth, n6 // tn),
        in_specs=[pl.BlockSpec((rows, d), lambda l, j: (0, 0)),
                  pl.BlockSpec((None, d, tn), lambda l, j: (l, 0, j)),
                  pl.BlockSpec((None, 1, tn), lambda l, j: (l, 0, j))],
        out_specs=pl.BlockSpec((None, rows, tn), lambda l, j: (l, 0, j)),
        compiler_params=_cparams(("arbitrary", "arbitrary")),
        name="ada_mod",
    )(c_pad, w_ada, b_ada.reshape(depth, 1, n6))


def _inproj_even_kernel(x_ref, g_ref, sc_ref, sh_ref, w_ref, bd_ref, qg_ref, kg_ref,
                        qb_ref, kf_ref, kb_ref, vf_ref, vb_ref, u_ref, *, aw, bw):
    h = _modnorm(x_ref[...], g_ref[...], sc_ref[...], sh_ref[...]).astype(BF16)
    bd = bd_ref[...]

    def group_norm(t, gain):
        tt = t * t
        hi = tt.astype(BF16)
        lo = (tt - hi.astype(F32)).astype(BF16)
        ss = _dot(hi, bd) + _dot(lo, bd)
        return t * lax.rsqrt(ss * (1.0 / HD_A) + EPS) * gain

    q = group_norm(_dot(h, w_ref[:, 0:aw]), qg_ref[...])
    qb_ref[...] = (q * (HD_A ** -0.5)).astype(BF16)
    k = group_norm(_dot(h, w_ref[:, aw:2 * aw]), kg_ref[...])
    kf_ref[...] = k
    kb_ref[...] = k.astype(BF16)
    v = _dot(h, w_ref[:, 2 * aw:3 * aw])
    vf_ref[...] = v
    vb_ref[...] = v.astype(BF16)
    ga = _dot(h, w_ref[:, 3 * aw:3 * aw + bw])
    gb = _dot(h, w_ref[:, 3 * aw + bw:3 * aw + 2 * bw])
    u_ref[...] = ga * jax.nn.sigmoid(gb)


def _inproj_even(x, tm, g, sc, sh, mod_map, w_bf, bd, qg, kg, aw, bw):
    rows, d = x.shape
    nt = rows // tm
    mr = sc.shape[1]
    row_spec = lambda width: pl.BlockSpec((tm, width), lambda i: (i, 0))
    const = lambda a: pl.BlockSpec(a.shape, lambda i: (0,) * a.ndim)
    mod_spec = pl.BlockSpec((None, mr, d), lambda i: (mod_map(i), 0, 0))
    outs = [jax.ShapeDtypeStruct((rows, aw), BF16), jax.ShapeDtypeStruct((rows, aw), F32),
            jax.ShapeDtypeStruct((rows, aw), BF16), jax.ShapeDtypeStruct((rows, aw), F32),
            jax.ShapeDtypeStruct((rows, aw), BF16), jax.ShapeDtypeStruct((rows, bw), F32)]
    return pl.pallas_call(
        functools.partial(_inproj_even_kernel, aw=aw, bw=bw),
        out_shape=outs,
        grid=(nt,),
        in_specs=[pl.BlockSpec((tm, d), lambda i: (i, 0)), const(g), mod_spec, mod_spec,
                  const(w_bf), const(bd), const(qg), const(kg)],
        out_specs=[row_spec(aw)] * 5 + [row_spec(bw)],
        compiler_params=_cparams(("arbitrary",)),
        name="inproj_even",
    )(x, g, sc, sh, w_bf, bd, qg, kg)


def _lambda_from(lam_ref, lam_init):
    lv = lam_ref[...]
    a = jnp.sum(lv[0:1] * lv[1:2], axis=-1, keepdims=True)
    b = jnp.sum(lv[2:3] * lv[3:4], axis=-1, keepdims=True)
    return jnp.exp(a) - jnp.exp(b) + lam_init


def _alibi_slope(head, n_heads):
    slope = jnp.float32(0.0)
    for hh in range(n_heads):
        slope = jnp.where(head == hh, jnp.float32(2.0 ** (-ALIBI_MAX * (hh + 1) / n_heads)), slope)
    return slope


def _attn_prompt_kernel(lam_ref, q_ref, k_ref, v_ref, sg_ref, o_ref, m_ref, l_ref, acc_ref,
                        *, tq, lam_init, n_heads):
    h = pl.program_id(1)
    qi = pl.program_id(2)
    kj = pl.program_id(3)

    @pl.when(kj == 0)
    def _():
        m_ref[...] = jnp.full(m_ref.shape, NEG_BIG, F32)
        l_ref[...] = jnp.zeros(l_ref.shape, F32)
        acc_ref[...] = jnp.zeros(acc_ref.shape, F32)

    slope = _alibi_slope(h, n_heads)

    def step(masked):
        q = q_ref[...]
        k = k_ref[...]
        v = v_ref[...]
        row = lax.broadcasted_iota(jnp.int32, (tq, tq), 0)
        col = lax.broadcasted_iota(jnp.int32, (tq, tq), 1)
        dist = (qi - kj) * tq + row - col
        bias = -slope * dist.astype(F32)
        lane = lax.broadcasted_iota(jnp.int32, q.shape, 1)
        zero = jnp.zeros_like(q)
        for m in range(2):
            qm = jnp.where((lane < HD_A) if m == 0 else (lane >= HD_A), q, zero)
            s = _dot_nt(qm, k) + bias
            if masked:
                s = jnp.where(dist >= 0, s, NEG_BIG)
            m_old = m_ref[m]
            m_new = jnp.maximum(m_old, jnp.max(s, axis=-1, keepdims=True))
            alpha = jnp.exp(m_old - m_new)
            p = jnp.exp(s - m_new)
            l_ref[m] = alpha * l_ref[m] + jnp.sum(p, axis=-1, keepdims=True)
            acc_ref[m] = alpha * acc_ref[m] + _dot(p.astype(BF16), v)
            m_ref[m] = m_new

    @pl.when(kj < qi)
    def _():
        step(False)

    @pl.when(kj == qi)
    def _():
        step(True)
        lam = _lambda_from(lam_ref, lam_init)
        o = acc_ref[0] / l_ref[0] - lam * (acc_ref[1] / l_ref[1])
        o = o * lax.rsqrt(jnp.mean(o * o, axis=-1, keepdims=True) + EPS)
        o_ref[...] = (o * sg_ref[...] * (1.0 - lam_init)).astype(o_ref.dtype)


def _attn_prompt(lamv, qb, kb, vb, sg, batch, seq, lam_init):
    n_heads = qb.shape[1] // DK_A
    tq = _largest_tile(seq, 512, LANES)
    nq = seq // tq
    q_spec = pl.BlockSpec((tq, DK_A), lambda b, h, i, j: (b * nq + i, h))
    kv_spec = pl.BlockSpec((tq, DK_A), lambda b, h, i, j: (b * nq + jnp.minimum(i, j), h))
    return pl.pallas_call(
        functools.partial(_attn_prompt_kernel, tq=tq, lam_init=lam_init, n_heads=n_heads),
        out_shape=jax.ShapeDtypeStruct(qb.shape, BF16),
        grid=(batch, n_heads, nq, nq),
        in_specs=[pl.BlockSpec(lamv.shape, lambda b, h, i, j: (0, 0)), q_spec, kv_spec, kv_spec,
                  pl.BlockSpec(sg.shape, lambda b, h, i, j: (0, 0))],
        out_specs=q_spec,
        scratch_shapes=[pltpu.VMEM((2, tq, 1), F32), pltpu.VMEM((2, tq, 1), F32),
                        pltpu.VMEM((2, tq, DK_A), F32)],
        compiler_params=_cparams(("arbitrary",) * 4),
        name="attn_prompt",
    )(lamv, qb, kb, vb, sg)


def _attn_sample_kernel(pt_ref, lam_ref, q_ref, *rest, pages, page, past, t_dec, n_heads, lam_init):
    k_refs = rest[:pages]
    v_refs = rest[pages:2 * pages]
    kn_ref, vn_ref, sg_ref, o_ref, m_ref, l_ref, acc_ref = rest[2 * pages:]
    j = pl.program_id(1)
    nrow = 2 * n_heads * t_dec

    @pl.when(j == 0)
    def _():
        m_ref[...] = jnp.full(m_ref.shape, NEG_BIG, F32)
        l_ref[...] = jnp.zeros(l_ref.shape, F32)
        acc_ref[...] = jnp.zeros(acc_ref.shape, F32)

    q = q_ref[...]
    r = lax.broadcasted_iota(jnp.int32, (nrow, 1), 0)
    hrow = (r % (n_heads * t_dec)) // t_dec
    trow = r % t_dec
    slope = _alibi_slope(hrow, n_heads)

    def update(s, v):
        m_old = m_ref[...]
        m_new = jnp.maximum(m_old, jnp.max(s, axis=-1, keepdims=True))
        alpha = jnp.exp(m_old - m_new)
        p = jnp.exp(s - m_new)
        l_ref[...] = alpha * l_ref[...] + jnp.sum(p, axis=-1, keepdims=True)
        acc_ref[...] = alpha * acc_ref[...] + _dot(p.astype(BF16), v)
        m_ref[...] = m_new

    col = lax.broadcasted_iota(jnp.int32, (nrow, page), 1)
    for p in range(pages):
        kpos0 = (j * pages + p) * page
        dist = (past + trow - kpos0) - col
        s = _dot_nt(q, k_refs[p][...].astype(BF16)) - slope * dist.astype(F32)
        update(s, v_refs[p][...].astype(BF16))

    @pl.when(j == pl.num_programs(1) - 1)
    def _():
        dist = trow - col
        s = _dot_nt(q, kn_ref[...]) - slope * dist.astype(F32)
        s = jnp.where((dist >= 0) & (col < t_dec), s, NEG_BIG)
        update(s, vn_ref[...])
        lam = _lambda_from(lam_ref, lam_init)
        acc = acc_ref[...] / l_ref[...]
        half = nrow // 2
        dmap = acc[0:half] - lam * acc[half:nrow]
        lane_h = lax.broadcasted_iota(jnp.int32, dmap.shape, 1) // DK_A
        row_h = lax.broadcasted_iota(jnp.int32, dmap.shape, 0) // t_dec
        dmap = jnp.where(lane_h == row_h, dmap, 0.0)
        o = dmap[0:t_dec]
        for hh in range(1, n_heads):
            o = o + dmap[hh * t_dec:(hh + 1) * t_dec]
        outs = []
        for hh in range(n_heads):
            c = o[:, hh * DK_A:(hh + 1) * DK_A]
            c = c * lax.rsqrt(jnp.mean(c * c, axis=-1, keepdims=True) + EPS)
            outs.append(c * sg_ref[...] * (1.0 - lam_init))
        o_ref[...] = jnp.concatenate(outs, axis=-1)


def _attn_sample(pt_flat, lamv, qbd, cache_k2, cache_v2, kn, vn, sg, n_pages, t_dec, past, lam_init):
    bs, nrow, aw = qbd.shape
    page = cache_k2.shape[1]
    n_heads = aw // DK_A
    pages = _largest_tile(n_pages, 8, 1)
    steps = n_pages // pages

    def page_spec(p):
        return pl.BlockSpec((None, page, aw),
                            lambda b, j, pt: (pt[b * n_pages + j * pages + p], 0, 0))

    per_b = lambda a: pl.BlockSpec((None,) + a.shape[1:], lambda b, j, pt: (b, 0, 0))
    const = lambda a: pl.BlockSpec(a.shape, lambda b, j, pt: (0, 0))
    grid_spec = pltpu.PrefetchScalarGridSpec(
        num_scalar_prefetch=1,
        grid=(bs, steps),
        in_specs=[const(lamv), per_b(qbd)] + [page_spec(p) for p in range(pages)]
                 + [page_spec(p) for p in range(pages)] + [per_b(kn), per_b(vn), const(sg)],
        out_specs=pl.BlockSpec((None, t_dec, aw), lambda b, j, pt: (b, 0, 0)),
        scratch_shapes=[pltpu.VMEM((nrow, 1), F32), pltpu.VMEM((nrow, 1), F32),
                        pltpu.VMEM((nrow, aw), F32)],
    )
    return pl.pallas_call(
        functools.partial(_attn_sample_kernel, pages=pages, page=page, past=past, t_dec=t_dec,
                          n_heads=n_heads, lam_init=lam_init),
        out_shape=jax.ShapeDtypeStruct((bs, t_dec, aw), F32),
        grid_spec=grid_spec,
        compiler_params=_cparams(("arbitrary", "arbitrary")),
        name="attn_sample",
    )(pt_flat, lamv, qbd, *([cache_k2] * pages), *([cache_v2] * pages), kn, vn, sg)


CONV_B_HALO = 32


def _convb_kernel(u_ref, buf_ref, w_ref, b_ref, lg_ref, lb_ref, cv_ref, nb_ref, xp_ref, *, tm):
    t = pl.program_id(1)
    pad = CONV_B_HALO - (CONV_B_WIDTH - 1)

    @pl.when(t == 0)
    def _():
        xp_ref[0:CONV_B_HALO] = buf_ref[...]

    @pl.when(t > 0)
    def _():
        xp_ref[0:CONV_B_HALO] = xp_ref[tm:tm + CONV_B_HALO]

    xp_ref[CONV_B_HALO:CONV_B_HALO + tm] = u_ref[...]
    ch = min(64, tm)
    for c in range(tm // ch):
        acc = jnp.zeros((ch, u_ref.shape[1]), F32)
        for j in range(CONV_B_WIDTH):
            s0 = c * ch + pad + j
            acc = acc + w_ref[j:j + 1, :] * xp_ref[s0:s0 + ch, :]
        y = acc + b_ref[...]
        mu = jnp.mean(y, axis=-1, keepdims=True)
        yc = y - mu
        z = yc * lax.rsqrt(jnp.mean(yc * yc, axis=-1, keepdims=True) + EPS) * lg_ref[...] + lb_ref[...]
        cv_ref[c * ch:(c + 1) * ch] = (z * jax.nn.sigmoid(z)).astype(cv_ref.dtype)
    nb_ref[...] = xp_ref[tm:tm + CONV_B_HALO]


def _convb(u3, buf_pad, w_pad, b, lg, lb, out_dtype):
    nb, t, c = u3.shape
    tm = t if t < SUBLANES else _largest_tile(t, 256)
    nt = t // tm
    const = lambda a: pl.BlockSpec(a.shape, lambda bb, tt: (0, 0))
    return pl.pallas_call(
        functools.partial(_convb_kernel, tm=tm),
        out_shape=[jax.ShapeDtypeStruct((nb, t, c), out_dtype),
                   jax.ShapeDtypeStruct((nb, CONV_B_HALO, c), F32)],
        grid=(nb, nt),
        in_specs=[pl.BlockSpec((None, tm, c), lambda bb, tt: (bb, tt, 0)),
                  pl.BlockSpec((None, CONV_B_HALO, c), lambda bb, tt: (bb, 0, 0)),
                  const(w_pad), const(b), const(lg), const(lb)],
        out_specs=[pl.BlockSpec((None, tm, c), lambda bb, tt: (bb, tt, 0)),
                   pl.BlockSpec((None, CONV_B_HALO, c), lambda bb, tt: (bb, 0, 0))],
        scratch_shapes=[pltpu.VMEM((tm + CONV_B_HALO, c), F32)],
        compiler_params=_cparams(("arbitrary", "arbitrary")),
        name="conv_b",
    )(u3, buf_pad, w_pad, b, lg, lb)


def _outproj_even_kernel(o_ref, cv_ref, w_ref, x_ref, g1_ref, gf_ref, sc2_ref, sh2_ref,
                         x1_ref, h2_ref, *, aw):
    mix = (_dot(o_ref[...].astype(BF16), w_ref[0:aw, :])
           + _dot(cv_ref[...].astype(BF16), w_ref[aw:, :]))
    x1 = x_ref[...] + g1_ref[...] * mix
    x1_ref[...] = x1
    h2_ref[...] = _modnorm(x1, gf_ref[...], sc2_ref[...], sh2_ref[...])


def _outproj_even(o, cv, w_bf, x, tm, g1, gf, sc2, sh2, mod_map):
    rows, aw = o.shape
    d = x.shape[1]
    mr = g1.shape[1]
    const = lambda a: pl.BlockSpec(a.shape, lambda i: (0,) * a.ndim)
    mod_spec = pl.BlockSpec((None, mr, d), lambda i: (mod_map(i), 0, 0))
    xrow = pl.BlockSpec((tm, d), lambda i: (i, 0))
    return pl.pallas_call(
        functools.partial(_outproj_even_kernel, aw=aw),
        out_shape=[jax.ShapeDtypeStruct((rows, d), F32)] * 2,
        grid=(rows // tm,),
        in_specs=[pl.BlockSpec((tm, aw), lambda i: (i, 0)),
                  pl.BlockSpec((tm, cv.shape[1]), lambda i: (i, 0)),
                  const(w_bf), xrow, mod_spec, const(gf), mod_spec, mod_spec],
        out_specs=[xrow, xrow],
        compiler_params=_cparams(("arbitrary",)),
        name="outproj_even",
    )(o, cv, w_bf, x, g1, gf, sc2, sh2)


CONV_C_HALO = 8


def _odd_kernel(x_ref, gm_ref, sc1_ref, sh1_ref, win_ref, cw_ref, halo_ref, wout_ref, g1_ref,
                gf_ref, sc2_ref, sh2_ref, x1_ref, h2_ref, nb_ref, zp_ref,
                *, tm, cwid, seq_tiled, t_dec):
    x = x_ref[...]
    h = _modnorm(x, gm_ref[...], sc1_ref[...], sh1_ref[...]).astype(BF16)
    cg = _dot(h, win_ref[:, cwid:2 * cwid])
    hx = _dot(h, win_ref[:, 2 * cwid:3 * cwid])
    z = cg * hx
    halo = CONV_C_HALO
    if seq_tiled:
        t = pl.program_id(1)

        @pl.when(t == 0)
        def _():
            zp_ref[0:halo] = halo_ref[...]

        @pl.when(t > 0)
        def _():
            zp_ref[0:halo] = zp_ref[tm:tm + halo]
    else:
        zp_ref[0:halo] = jnp.zeros((halo, cwid), F32)
    zp_ref[halo:halo + tm] = z
    prev1 = zp_ref[halo - 1:halo - 1 + tm]
    prev2 = zp_ref[halo - 2:halo - 2 + tm]
    if seq_tiled:
        nb_ref[...] = zp_ref[tm:tm + halo]
    else:
        tpos = lax.broadcasted_iota(jnp.int32, (tm, 1), 0) % t_dec
        prev1 = jnp.where(tpos >= 1, prev1, halo_ref[0])
        prev2 = jnp.where(tpos >= 2, prev2, halo_ref[1])
        nb_ref[...] = z
    y = cw_ref[0:1, :] * prev2 + cw_ref[1:2, :] * prev1 + cw_ref[2:3, :] * z
    bg = _dot(h, win_ref[:, 0:cwid])
    mix = _dot((bg * y).astype(BF16), wout_ref[...])
    x1 = x + g1_ref[...] * mix
    x1_ref[...] = x1
    h2_ref[...] = _modnorm(x1, gf_ref[...], sc2_ref[...], sh2_ref[...])


def _odd_mixer(x, nb, t, tm, gm, sc1, sh1, mod_map, win_bf, cw, halo, wout_bf, g1,
               gf, sc2, sh2, seq_tiled, t_dec):
    rows, d = x.shape
    cwid = wout_bf.shape[0]
    mr = g1.shape[1]
    if seq_tiled:
        nt = t // tm
        grid = (nb, nt)
        xmap = lambda b, i: (b * nt + i, 0)
        halo_spec = pl.BlockSpec((None, CONV_C_HALO, cwid), lambda b, i: (b, 0, 0))
        nb_shape = jax.ShapeDtypeStruct((nb, CONV_C_HALO, cwid), F32)
        nb_spec = pl.BlockSpec((None, CONV_C_HALO, cwid), lambda b, i: (b, 0, 0))
        mmap = lambda b, i: (mod_map(b, i), 0, 0)
    else:
        grid = (1, 1)
        xmap = lambda b, i: (0, 0)
        halo_spec = pl.BlockSpec(halo.shape, lambda b, i: (0, 0, 0))
        nb_shape = jax.ShapeDtypeStruct((tm, cwid), F32)
        nb_spec = pl.BlockSpec((tm, cwid), lambda b, i: (0, 0))
        mmap = lambda b, i: (0, 0, 0)
    const = lambda a: pl.BlockSpec(a.shape, lambda b, i: (0,) * a.ndim)
    mod_spec = pl.BlockSpec((None, mr, d), mmap)
    xrow = pl.BlockSpec((tm, d), xmap)
    return pl.pallas_call(
        functools.partial(_odd_kernel, tm=tm, cwid=cwid, seq_tiled=seq_tiled, t_dec=t_dec),
        out_shape=[jax.ShapeDtypeStruct((rows, d), F32)] * 2 + [nb_shape],
        grid=grid,
        in_specs=[xrow, const(gm), mod_spec, mod_spec, const(win_bf), const(cw), halo_spec,
                  const(wout_bf), mod_spec, const(gf), mod_spec, mod_spec],
        out_specs=[xrow, xrow, nb_spec],
        scratch_shapes=[pltpu.VMEM((tm + CONV_C_HALO, cwid), F32)],
        compiler_params=_cparams(("arbitrary", "arbitrary")),
        name="odd_mixer",
    )(x, gm, sc1, sh1, win_bf, cw, halo, wout_bf, g1, gf, sc2, sh2)


def _route_kernel(h_ref, w0_ref, w1_ref, w2_ref, b_ref, tri_ref, cnt0_ref, meta_ref, cnt_ref, carry_ref):
    i = pl.program_id(0)

    @pl.when(i == 0)
    def _():
        carry_ref[...] = cnt0_ref[...]

    a0, a1, a2 = _split3(h_ref[...])
    w0, w1, w2 = w0_ref[...], w1_ref[...], w2_ref[...]
    logits = (_dot(a0, w0) + (_dot(a0, w1) + _dot(a1, w0))
              + (_dot(a1, w1) + _dot(a0, w2) + _dot(a2, w0))) + b_ref[...]
    tm = logits.shape[0]
    lane = lax.broadcasted_iota(jnp.int32, logits.shape, 1)
    lane_f = lane.astype(F32)
    neg = jnp.float32(-jnp.inf)

    def first_max(vals):
        vmax = jnp.max(vals, axis=-1, keepdims=True)
        idx = jnp.min(jnp.where(vals == vmax, lane_f, float(LANES)), axis=-1, keepdims=True)
        return vmax, idx.astype(jnp.int32)

    gmask = lane < N_GROUPS
    gmax, g_idx = first_max(jnp.where(gmask, logits, neg))
    g_w = 1.0 / jnp.sum(jnp.where(gmask, jnp.exp(logits - gmax), 0.0), axis=-1, keepdims=True)
    lo = N_GROUPS + EXPERTS_PER_GROUP * g_idx
    el = jnp.where((lane >= lo) & (lane < lo + EXPERTS_PER_GROUP), logits, neg)
    v1, i1 = first_max(el)
    v2, i2 = first_max(jnp.where(lane == i1, neg, el))
    tt = jnp.exp(v2 - v1)
    wgt1 = g_w / (1.0 + tt)
    wgt2 = g_w * tt / (1.0 + tt)
    e1 = i1 - N_GROUPS
    e2 = i2 - N_GROUPS
    hit1 = lane == e1
    hit2 = lane == e2
    onehot = (hit1 | hit2).astype(F32)
    cnt = _dot(tri_ref[...], onehot.astype(BF16)) + carry_ref[...]
    r1 = jnp.sum(jnp.where(hit1, cnt, 0.0), axis=-1, keepdims=True)
    r2 = jnp.sum(jnp.where(hit2, cnt, 0.0), axis=-1, keepdims=True)
    carry_ref[...] = carry_ref[...] + jnp.sum(onehot, axis=0, keepdims=True)
    cnt_ref[...] = carry_ref[...]
    cols = (e1.astype(F32), e2.astype(F32), r1, r2, wgt1, wgt2)
    meta = jnp.zeros((tm, LANES), F32)
    for c, val in enumerate(cols):
        meta = jnp.where(lane == c, val, meta)
    meta_ref[...] = meta


def _route(h2, w_parts, b_pad, cnt0, tm):
    n, d = h2.shape
    tri = (jnp.arange(tm)[:, None] > jnp.arange(tm)[None, :]).astype(BF16)
    const = lambda a: pl.BlockSpec(a.shape, lambda i: (0, 0))
    return pl.pallas_call(
        _route_kernel,
        out_shape=[jax.ShapeDtypeStruct((n, LANES), F32), jax.ShapeDtypeStruct((1, LANES), F32)],
        grid=(n // tm,),
        in_specs=[pl.BlockSpec((tm, d), lambda i: (i, 0)), const(w_parts[0]), const(w_parts[1]),
                  const(w_parts[2]), const(b_pad), const(tri), const(cnt0)],
        out_specs=[pl.BlockSpec((tm, LANES), lambda i: (i, 0)), const(cnt0)],
        scratch_shapes=[pltpu.VMEM((1, LANES), F32)],
        compiler_params=_cparams(("arbitrary",)),
        name="moe_route",
    )(h2, *w_parts, b_pad, tri, cnt0)


def _dispatch_kernel(dest_ref, h_ref, xp_in_ref, xp_ref, sem, *, tm):
    del xp_in_ref

    def row_copy(r, slot):
        return pltpu.make_async_copy(h_ref.at[pl.ds(r, 1)], xp_ref.at[pl.ds(slot, 1)], sem)

    def issue(r, c):
        for k in range(TOP_K):
            row_copy(r, dest_ref[0, TOP_K * r + k]).start()
        return c

    lax.fori_loop(0, tm, issue, 0)

    def drain(r, c):
        for k in range(TOP_K):
            row_copy(r, dest_ref[0, TOP_K * r + k]).wait()
        return c

    lax.fori_loop(0, tm, drain, 0)


def _dispatch(dest3, h2_all, xp_zero, tm):
    n, d = h2_all.shape
    return pl.pallas_call(
        functools.partial(_dispatch_kernel, tm=tm),
        out_shape=jax.ShapeDtypeStruct(xp_zero.shape, F32),
        grid=(n // tm,),
        in_specs=[pl.BlockSpec((None, 1, TOP_K * tm), lambda i: (i, 0, 0), memory_space=pltpu.SMEM),
                  pl.BlockSpec((tm, d), lambda i: (i, 0)),
                  pl.BlockSpec(memory_space=pl.ANY)],
        out_specs=pl.BlockSpec(memory_space=pl.ANY),
        input_output_aliases={2: 0},
        scratch_shapes=[pltpu.SemaphoreType.DMA],
        compiler_params=_cparams(("arbitrary",)),
        name="moe_dispatch",
    )(dest3, h2_all, xp_zero)


def _expert_kernel(be_ref, nu_ref, x_ref, wgu_ref, wd_ref, y_ref, wgu_bf, wd_bf, *, f):
    b = pl.program_id(0)
    prev = be_ref[jnp.maximum(b - 1, 0)]

    @pl.when((b == 0) | (be_ref[b] != prev))
    def _():
        wgu_bf[...] = wgu_ref[...].astype(BF16)
        wd_bf[...] = wd_ref[...].astype(BF16)

    @pl.when(b < nu_ref[0])
    def _():
        x = x_ref[...].astype(BF16)
        gate = _dot(x, wgu_bf[:, 0:f])
        up = _dot(x, wgu_bf[:, f:2 * f])
        act = (gate * jax.nn.sigmoid(gate) * up).astype(BF16)
        y_ref[...] = _dot(act, wd_bf[...])

    @pl.when(b >= nu_ref[0])
    def _():
        y_ref[...] = jnp.zeros(y_ref.shape, F32)


def _experts(block_e, n_used, xp, w_gate_up, w_down, layer):
    n_slots, d = xp.shape
    f = w_down.shape[2]
    grid_spec = pltpu.PrefetchScalarGridSpec(
        num_scalar_prefetch=2,
        grid=(n_slots // MOE_ROWS,),
        in_specs=[pl.BlockSpec((MOE_ROWS, d), lambda b, be, nu: (b, 0)),
                  pl.BlockSpec((None, None, d, 2 * f), lambda b, be, nu: (layer, be[b], 0, 0)),
                  pl.BlockSpec((None, None, f, d), lambda b, be, nu: (layer, be[b], 0, 0))],
        out_specs=pl.BlockSpec((MOE_ROWS, d), lambda b, be, nu: (b, 0)),
        scratch_shapes=[pltpu.VMEM((d, 2 * f), BF16), pltpu.VMEM((f, d), BF16)],
    )
    return pl.pallas_call(
        functools.partial(_expert_kernel, f=f),
        out_shape=jax.ShapeDtypeStruct((n_slots, d), F32),
        grid_spec=grid_spec,
        compiler_params=_cparams(("arbitrary",)),
        name="moe_experts",
    )(block_e, n_used, xp, w_gate_up, w_down)


def _combine_kernel(dest_ref, x_ref, meta_ref, g2_ref, yp_ref, o_ref, ybuf, sem, *, tm):

    def row_copy(r, k, slot):
        return pltpu.make_async_copy(yp_ref.at[pl.ds(slot, 1)], ybuf.at[k, pl.ds(r, 1)], sem)

    def issue(r, c):
        for k in range(TOP_K):
            row_copy(r, k, dest_ref[0, TOP_K * r + k]).start()
        return c

    lax.fori_loop(0, tm, issue, 0)

    def drain(r, c):
        for k in range(TOP_K):
            row_copy(r, k, dest_ref[0, TOP_K * r + k]).wait()
        return c

    lax.fori_loop(0, tm, drain, 0)
    meta = meta_ref[...]
    y = meta[:, 4:5] * ybuf[0] + meta[:, 5:6] * ybuf[1]
    o_ref[...] = x_ref[...] + g2_ref[...] * y


def _combine(dest3, x1, meta, g2, mod_map, yp, tm):
    rows, d = x1.shape
    mr = g2.shape[1]
    xrow = pl.BlockSpec((tm, d), lambda i: (i, 0))
    return pl.pallas_call(
        functools.partial(_combine_kernel, tm=tm),
        out_shape=jax.ShapeDtypeStruct((rows, d), F32),
        grid=(rows // tm,),
        in_specs=[pl.BlockSpec((None, 1, TOP_K * tm), lambda i: (i, 0, 0), memory_space=pltpu.SMEM),
                  xrow, pl.BlockSpec((tm, LANES), lambda i: (i, 0)),
                  pl.BlockSpec((None, mr, d), lambda i: (mod_map(i), 0, 0)),
                  pl.BlockSpec(memory_space=pl.ANY)],
        out_specs=xrow,
        scratch_shapes=[pltpu.VMEM((TOP_K, tm, d), F32), pltpu.SemaphoreType.DMA],
        compiler_params=_cparams(("arbitrary",)),
        name="moe_combine",
    )(dest3, x1, meta, g2, yp)


def _moe(h2_p, h2_s, x1_p, x1_s, g2_p, g2_s, w_parts, b_pad, w_gate_up, w_down, layer, tm_p, seq):
    n_p, d = h2_p.shape
    n_s = h2_s.shape[0]
    n = n_p + n_s
    meta_p, cnt_p = _route(h2_p, w_parts, b_pad, jnp.zeros((1, LANES), F32), tm_p)
    meta_s, counts = _route(h2_s, w_parts, b_pad, cnt_p, n_s)
    sizes = counts[0, :N_EXPERTS].astype(jnp.int32)
    padded = ((sizes + MOE_ROWS - 1) // MOE_ROWS) * MOE_ROWS
    pad_end = jnp.cumsum(padded)
    pad_start = pad_end - padded

    def slots(meta):
        e_tok = meta[:, 0:TOP_K].astype(jnp.int32)
        rank = meta[:, TOP_K:2 * TOP_K].astype(jnp.int32)
        start = jnp.sum(jnp.where(e_tok[:, :, None] == jnp.arange(N_EXPERTS)[None, None, :],
                                  pad_start[None, None, :], 0), axis=-1)
        return (start + rank).astype(jnp.int32)

    dest_p = slots(meta_p).reshape(n_p // tm_p, 1, TOP_K * tm_p)
    dest_s = slots(meta_s).reshape(1, 1, TOP_K * n_s)
    n_blocks = (n * TOP_K + MOE_ROWS - 1) // MOE_ROWS + N_EXPERTS
    blk0 = jnp.arange(n_blocks, dtype=jnp.int32) * MOE_ROWS
    block_e = jnp.minimum(jnp.sum(pad_end[None, :] <= blk0[:, None], axis=-1), N_EXPERTS - 1).astype(jnp.int32)
    n_used = (pad_end[-1:] // MOE_ROWS).astype(jnp.int32)

    xp = _dispatch(dest_p, h2_p, jnp.zeros((n_blocks * MOE_ROWS, d), F32), tm_p)
    xp = _dispatch(dest_s, h2_s, xp, n_s)
    yp = _experts(block_e, n_used, xp, w_gate_up, w_down, layer)
    per_seq = seq // tm_p
    out_p = _combine(dest_p, x1_p, meta_p, g2_p, lambda i: i // per_seq, yp, tm_p)
    out_s = _combine(dest_s, x1_s, meta_s, g2_s, lambda i: 0, yp, n_s)
    return out_p, out_s


def kernel(x_prompt, x_sample, cache_k, cache_v, state_conv_b, state_conv_c, page_table, c_prompt, c_sample, norm_mix_g, norm_ffn_g, w_ada, b_ada, w_in_a, q_norm_g, k_norm_g, lam_q1, lam_k1, lam_q2, lam_k2, subln_g, convb_w, convb_b, convb_ln_g, convb_ln_b, w_out_a, w_in_c, convc_w, w_out_c, w_group, b_group, w_router, b_router, w_gate_up, w_down):
    bp, seq, d = x_prompt.shape
    bs, t_dec, _ = x_sample.shape
    depth = w_ada.shape[0]
    n_p, n_s = bp * seq, bs * t_dec
    n = n_p + n_s
    n_pool, page = cache_k.shape[1], cache_k.shape[2]
    n_heads = cache_k.shape[3]
    aw = n_heads * DK_A
    bw = state_conv_b.shape[-1]
    cwid = state_conv_c.shape[-1]
    n_pages = page_table.shape[1]
    past = n_pages * page
    tm_p = _largest_tile(seq, 512, LANES)
    assert n_s % SUBLANES == 0

    x_p = x_prompt.reshape(n_p, d)
    x_s = x_sample.reshape(n_s, d)

    c_all = jnp.concatenate([c_prompt, c_sample], axis=0)
    rows = c_all.shape[0]
    rows_pad = -(-rows // SUBLANES) * SUBLANES
    mods = _ada_all(jnp.pad(c_all, ((0, rows_pad - rows), (0, 0))), w_ada, b_ada)
    mods = mods[:, :rows].reshape(depth, rows, 6, d)

    def mod_rows(l, j):
        m_p = mods[l, :bp, j][:, None, :]
        m_s = jnp.repeat(mods[l, bp:, j], t_dec, axis=0)[None]
        return m_p, m_s

    bd = (jnp.arange(aw)[:, None] // HD_A == jnp.arange(aw)[None, :] // HD_A).astype(BF16)
    cache_k2 = cache_k.reshape(cache_k.shape[0] * n_pool, page, aw)
    cache_v2 = cache_v.reshape(cache_v.shape[0] * n_pool, page, aw)
    per_seq = seq // tm_p
    pmap1 = lambda i: i // per_seq
    smap1 = lambda i: 0
    row2 = lambda a: a.reshape(1, -1)

    k_p, v_p, k_s, v_s, cb_p, cb_s, cc_p, cc_s = [], [], [], [], [], [], [], []
    for l in range(depth):
        i = l // 2
        sh1_p, sh1_s = mod_rows(l, 0)
        sc1_p, sc1_s = mod_rows(l, 1)
        g1_p, g1_s = mod_rows(l, 2)
        sh2_p, sh2_s = mod_rows(l, 3)
        sc2_p, sc2_s = mod_rows(l, 4)
        g2_p, g2_s = mod_rows(l, 5)
        gm = row2(norm_mix_g[l])
        gf = row2(norm_ffn_g[l])
        if l % 2 == 0:
            lam_init = 0.8 - 0.6 * math.exp(-0.3 * l)
            w_in = w_in_a[i].astype(BF16)
            w_out = w_out_a[i].astype(BF16)
            qg = jnp.tile(q_norm_g[i].reshape(1, DK_A), (1, n_heads))
            kg = jnp.tile(k_norm_g[i].reshape(1, DK_A), (1, n_heads))
            lamv = jnp.stack([lam_q1[i], lam_k1[i], lam_q2[i], lam_k2[i]])
            sg = row2(subln_g[i])
            cw = jnp.pad(convb_w[i].reshape(CONV_B_WIDTH, bw), ((0, 1), (0, 0)))
            cbb, clg, clb = row2(convb_b[i]), row2(convb_ln_g[i]), row2(convb_ln_b[i])
            halo_pad = ((0, 0), (CONV_B_HALO - (CONV_B_WIDTH - 1), 0), (0, 0))

            qb, kf, kb, vf, vb, u = _inproj_even(x_p, tm_p, gm, sc1_p, sh1_p, pmap1,
                                                 w_in, bd, qg, kg, aw, bw)
            o_p = _attn_prompt(lamv, qb, kb, vb, sg, bp, seq, lam_init)
            cv_p, nb_p = _convb(u.reshape(bp, seq, bw), jnp.zeros((bp, CONV_B_HALO, bw), F32),
                                cw, cbb, clg, clb, BF16)
            x1_p, h2_p = _outproj_even(o_p, cv_p.reshape(n_p, bw), w_out, x_p, tm_p,
                                       g1_p, gf, sc2_p, sh2_p, pmap1)
            k_p.append(kf.reshape(bp, seq, n_heads, DK_A))
            v_p.append(vf.reshape(bp, seq, n_heads, DK_A))
            cb_p.append(nb_p[:, CONV_B_HALO - (CONV_B_WIDTH - 1):])

            qb, kf, kb, vf, vb, u = _inproj_even(x_s, n_s, gm, sc1_s, sh1_s, smap1,
                                                 w_in, bd, qg, kg, aw, bw)
            q4 = qb.reshape(bs, 1, 1, t_dec, n_heads, 2, HD_A)
            eye_h = jnp.eye(n_heads, dtype=BF16)[None, None, :, None, :, None, None]
            eye_m = jnp.eye(2, dtype=BF16)[None, :, None, None, None, :, None]
            qbd = (q4 * eye_h * eye_m).reshape(bs, 2 * n_heads * t_dec, aw)
            kn = jnp.pad(kb.reshape(bs, t_dec, aw), ((0, 0), (0, page - t_dec), (0, 0)))
            vn = jnp.pad(vb.reshape(bs, t_dec, aw), ((0, 0), (0, page - t_dec), (0, 0)))
            pt_flat = (page_table.astype(jnp.int32) + i * n_pool).reshape(-1)
            o_s = _attn_sample(pt_flat, lamv, qbd, cache_k2, cache_v2, kn, vn, sg, n_pages, t_dec,
                               past, lam_init)
            cv_s, nb_s = _convb(u.reshape(bs, t_dec, bw), jnp.pad(state_conv_b[i], halo_pad),
                                cw, cbb, clg, clb, F32)
            x1_s, h2_s = _outproj_even(o_s.reshape(n_s, aw), cv_s.reshape(n_s, bw), w_out,
                                       x_s, n_s, g1_s, gf, sc2_s, sh2_s, smap1)
            k_s.append(kf.reshape(bs, t_dec, n_heads, DK_A))
            v_s.append(vf.reshape(bs, t_dec, n_heads, DK_A))
            cb_s.append(nb_s[:, CONV_B_HALO - (CONV_B_WIDTH - 1):])
        else:
            w_in = w_in_c[i].astype(BF16)
            w_out = w_out_c[i].astype(BF16)
            cw = jnp.pad(convc_w[i].reshape(CONV_C_WIDTH, cwid), ((0, SUBLANES - CONV_C_WIDTH), (0, 0)))
            halo_p = jnp.zeros((bp, CONV_C_HALO, cwid), F32)
            x1_p, h2_p, nb_p = _odd_mixer(
                x_p, bp, seq, tm_p, gm, sc1_p, sh1_p, lambda b, t: b, w_in, cw, halo_p, w_out,
                g1_p, gf, sc2_p, sh2_p, True, t_dec)
            cc_p.append(nb_p[:, CONV_C_HALO - (CONV_C_WIDTH - 1):])
            st = state_conv_c[i]
            zrow = jnp.zeros((bs, 1, cwid), F32)
            halo1 = jnp.concatenate([st[:, 1:2]] + [zrow] * (t_dec - 1), axis=1).reshape(n_s, cwid)
            halo2 = jnp.concatenate([st[:, 0:1], st[:, 1:2]] + [zrow] * (t_dec - 2), axis=1).reshape(n_s, cwid)
            x1_s, h2_s, z_s = _odd_mixer(
                x_s, 1, n_s, n_s, gm, sc1_s, sh1_s, None, w_in, cw, jnp.stack([halo1, halo2]),
                w_out, g1_s, gf, sc2_s, sh2_s, False, t_dec)
            cc_s.append(z_s.reshape(bs, t_dec, cwid)[:, t_dec - (CONV_C_WIDTH - 1):])

        w_r = jnp.pad(jnp.concatenate([w_group[l], w_router[l]], axis=1),
                      ((0, 0), (0, LANES - N_GROUPS - N_EXPERTS)))
        w0 = w_r.astype(BF16)
        rem = w_r - w0.astype(F32)
        w1 = rem.astype(BF16)
        w2 = (rem - w1.astype(F32)).astype(BF16)
        b_r = jnp.pad(jnp.concatenate([b_group[l], b_router[l]]), (0, LANES - N_GROUPS - N_EXPERTS))
        x_p, x_s = _moe(h2_p, h2_s, x1_p, x1_s, g2_p, g2_s, (w0, w1, w2), row2(b_r),
                        w_gate_up, w_down, l, tm_p, seq)

    return (x_p.reshape(bp, seq, d), x_s.reshape(bs, t_dec, d),
            jnp.stack(k_p), jnp.stack(v_p), jnp.stack(k_s), jnp.stack(v_s),
            jnp.stack(cb_p), jnp.stack(cb_s), jnp.stack(cc_p), jnp.stack(cc_s))
```

```python
import functools
import math

import jax
import jax.numpy as jnp
from jax import lax
from jax.experimental import pallas as pl
from jax.experimental.pallas import tpu as pltpu

F32 = jnp.float32
BF16 = jnp.bfloat16

HD_A = 64
DK_A = 2 * HD_A
N_GROUPS = 4
EXPERTS_PER_GROUP = 8
N_EXPERTS = N_GROUPS * EXPERTS_PER_GROUP
TOP_K = 2
CONV_B_WIDTH = 31
CONV_C_WIDTH = 3
ALIBI_MAX = 8.0
EPS = 1e-6
NEG_BIG = -1e30
LOG2E = math.log2(math.e)

LANES = 128
SUBLANES = 8
MOE_ROWS = 256
ATTN_TILE = 512
DMA_UNROLL = 8
VMEM_LIMIT = 56 * 1024 * 1024


def _cparams(sem, vmem=VMEM_LIMIT):
    return pltpu.CompilerParams(dimension_semantics=sem, vmem_limit_bytes=vmem)


def _largest_tile(n, cap, mult=SUBLANES):
    best = None
    for t in range(mult, min(n, cap) + 1, mult):
        if n % t == 0:
            best = t
    assert best is not None, (n, cap, mult)
    return best


def _modnorm(x, g, scale, shift):
    ms = jnp.mean(x * x, axis=-1, keepdims=True)
    return (x * lax.rsqrt(ms + EPS)) * g * (1.0 + scale) + shift


def _dot(a, b):
    return jnp.dot(a, b, preferred_element_type=F32)


def _dot_nt(a, b):
    return lax.dot_general(a, b, (((1,), (1,)), ((), ())), preferred_element_type=F32)


def _split3(a):
    a0 = a.astype(BF16)
    r1 = a - a0.astype(F32)
    a1 = r1.astype(BF16)
    a2 = (r1 - a1.astype(F32)).astype(BF16)
    return a0, a1, a2


def _ada_kernel(c_ref, w_ref, b_ref, o_ref):
    c = c_ref[...]
    a = (c * jax.nn.sigmoid(c)).astype(BF16)
    o_ref[...] = _dot(a, w_ref[...].astype(BF16)) + b_ref[...]


def _ada_all(c_pad, w_ada, b_ada):
    depth, d, n6 = w_ada.shape
    rows = c_pad.shape[0]
    tn = 1536
    return pl.pallas_call(
        _ada_kernel,
        out_shape=jax.ShapeDtypeStruct((depth, rows, n6), F32),
        grid=(depth, n6 // tn),
        in_specs=[pl.BlockSpec((rows, d), lambda l, j: (0, 0)),
                  pl.BlockSpec((None, d, tn), lambda l, j: (l, 0, j)),
                  pl.BlockSpec((None, 1, tn), lambda l, j: (l, 0, j))],
        out_specs=pl.BlockSpec((None, rows, tn), lambda l, j: (l, 0, j)),
        compiler_params=_cparams(("arbitrary", "arbitrary")),
        name="ada_mod",
    )(c_pad, w_ada, b_ada.reshape(depth, 1, n6))


def _inproj_even_kernel(x_ref, g_ref, sc_ref, sh_ref, w_ref, bd_ref, qg_ref, kg_ref,
                        qb_ref, kf_ref, kb_ref, vf_ref, vb_ref, u_ref, *, aw, bw, k_transposed):
    h = _modnorm(x_ref[...], g_ref[...], sc_ref[...], sh_ref[...]).astype(BF16)
    bd = bd_ref[...]

    def group_norm(t, gain):
        tt = t * t
        hi = tt.astype(BF16)
        lo = (tt - hi.astype(F32)).astype(BF16)
        ss = _dot(hi, bd) + _dot(lo, bd)
        return t * lax.rsqrt(ss * (1.0 / HD_A) + EPS) * gain

    q = group_norm(_dot(h, w_ref[:, 0:aw]), qg_ref[...])
    qb_ref[...] = (q * (HD_A ** -0.5 * LOG2E)).astype(BF16)
    k = group_norm(_dot(h, w_ref[:, aw:2 * aw]), kg_ref[...])
    kf_ref[...] = k
    kb_ref[...] = (k.T if k_transposed else k).astype(BF16)
    v = _dot(h, w_ref[:, 2 * aw:3 * aw])
    vf_ref[...] = v
    vb_ref[...] = v.astype(BF16)
    ga = _dot(h, w_ref[:, 3 * aw:3 * aw + bw])
    gb = _dot(h, w_ref[:, 3 * aw + bw:3 * aw + 2 * bw])
    u_ref[...] = ga * jax.nn.sigmoid(gb)


def _inproj_even(x, tm, g, sc, sh, mod_map, w_bf, bd, qg, kg, aw, bw, k_transposed):
    rows, d = x.shape
    nt = rows // tm
    mr = sc.shape[1]
    row_spec = lambda width: pl.BlockSpec((tm, width), lambda i: (i, 0))
    const = lambda a: pl.BlockSpec(a.shape, lambda i: (0,) * a.ndim)
    mod_spec = pl.BlockSpec((None, mr, d), lambda i: (mod_map(i), 0, 0))
    kb_shape = (aw, rows) if k_transposed else (rows, aw)
    kb_spec = pl.BlockSpec((aw, tm), lambda i: (0, i)) if k_transposed else row_spec(aw)
    outs = [jax.ShapeDtypeStruct((rows, aw), BF16), jax.ShapeDtypeStruct((rows, aw), F32),
            jax.ShapeDtypeStruct(kb_shape, BF16), jax.ShapeDtypeStruct((rows, aw), F32),
            jax.ShapeDtypeStruct((rows, aw), BF16), jax.ShapeDtypeStruct((rows, bw), F32)]
    return pl.pallas_call(
        functools.partial(_inproj_even_kernel, aw=aw, bw=bw, k_transposed=k_transposed),
        out_shape=outs,
        grid=(nt,),
        in_specs=[pl.BlockSpec((tm, d), lambda i: (i, 0)), const(g), mod_spec, mod_spec,
                  const(w_bf), const(bd), const(qg), const(kg)],
        out_specs=[row_spec(aw), row_spec(aw), kb_spec, row_spec(aw), row_spec(aw), row_spec(bw)],
        compiler_params=_cparams(("arbitrary",)),
        name="inproj_even",
    )(x, g, sc, sh, w_bf, bd, qg, kg)


def _lambda_from(lam_ref, lam_init):
    lv = lam_ref[...]
    a = jnp.sum(lv[0:1] * lv[1:2], axis=-1, keepdims=True)
    b = jnp.sum(lv[2:3] * lv[3:4], axis=-1, keepdims=True)
    return jnp.exp(a) - jnp.exp(b) + lam_init


def _alibi_slope(head, n_heads):
    slope = jnp.float32(0.0)
    for hh in range(n_heads):
        slope = jnp.where(head == hh, jnp.float32(2.0 ** (-ALIBI_MAX * (hh + 1) / n_heads)), slope)
    return slope


def _softmax_step(s, v_ones, m_ref, acc_ref):
    m_old = m_ref[...]
    m_new = jnp.maximum(m_old, jnp.max(s, axis=-1, keepdims=True))
    p = jnp.exp2(s - m_new)
    acc_ref[...] = jnp.exp2(m_old - m_new) * acc_ref[...] + _dot(p.astype(BF16), v_ones)
    m_ref[...] = m_new


def _with_ones(v):
    return jnp.concatenate([v, jnp.ones(v.shape, v.dtype)], axis=-1)


def _attn_prompt_kernel(lam_ref, q_ref, kt_ref, v_ref, sg_ref, o_ref, m1_ref, m2_ref, acc1_ref, acc2_ref,
                        *, tq, lam_init, n_heads):
    h = pl.program_id(1)
    qi = pl.program_id(2)
    for m_ref, acc_ref in ((m1_ref, acc1_ref), (m2_ref, acc2_ref)):
        m_ref[...] = jnp.full(m_ref.shape, NEG_BIG, F32)
        acc_ref[...] = jnp.zeros(acc_ref.shape, F32)
    slope2 = _alibi_slope(h, n_heads) * LOG2E
    q = q_ref[...]
    lane = lax.broadcasted_iota(jnp.int32, q.shape, 1)
    zero = jnp.zeros_like(q)
    q1 = jnp.where(lane < HD_A, q, zero)
    q2 = jnp.where(lane >= HD_A, q, zero)
    col = lax.broadcasted_iota(jnp.int32, (1, tq), 1)

    def step(kj, masked):
        r0 = pl.multiple_of(kj * tq, tq)
        kt = kt_ref[:, pl.ds(r0, tq)]
        v_ones = _with_ones(v_ref[pl.ds(r0, tq), :])
        colb = slope2 * ((kj - qi) * tq + col).astype(F32)
        if masked:
            keep = (lax.broadcasted_iota(jnp.int32, (tq, tq), 0)
                    >= lax.broadcasted_iota(jnp.int32, (tq, tq), 1))
        for qm, m_ref, acc_ref in ((q1, m1_ref, acc1_ref), (q2, m2_ref, acc2_ref)):
            s = _dot(qm, kt) + colb
            if masked:
                s = jnp.where(keep, s, NEG_BIG)
            _softmax_step(s, v_ones, m_ref, acc_ref)

    def body(kj, carry):
        step(kj, False)
        return carry

    lax.fori_loop(0, qi, body, 0)
    step(qi, True)
    lam = _lambda_from(lam_ref, lam_init)
    a1 = acc1_ref[...]
    a2 = acc2_ref[...]
    o = a1[:, :DK_A] / a1[:, DK_A:DK_A + 1] - lam * (a2[:, :DK_A] / a2[:, DK_A:DK_A + 1])
    o = o * lax.rsqrt(jnp.mean(o * o, axis=-1, keepdims=True) + EPS)
    o_ref[...] = (o * sg_ref[...] * (1.0 - lam_init)).astype(o_ref.dtype)


def _attn_prompt(lamv, qb, kbt, vb, sg, batch, seq, lam_init):
    n_heads = qb.shape[1] // DK_A
    tq = _largest_tile(seq, ATTN_TILE, LANES)
    nq = seq // tq
    q_spec = pl.BlockSpec((tq, DK_A), lambda b, h, i: (b * nq + i, h))
    kt_spec = pl.BlockSpec((DK_A, seq), lambda b, h, i: (h, b))
    kv_spec = pl.BlockSpec((seq, DK_A), lambda b, h, i: (b, h))
    return pl.pallas_call(
        functools.partial(_attn_prompt_kernel, tq=tq, lam_init=lam_init, n_heads=n_heads),
        out_shape=jax.ShapeDtypeStruct(qb.shape, BF16),
        grid=(batch, n_heads, nq),
        in_specs=[pl.BlockSpec(lamv.shape, lambda b, h, i: (0, 0)), q_spec, kt_spec, kv_spec,
                  pl.BlockSpec(sg.shape, lambda b, h, i: (0, 0))],
        out_specs=q_spec,
        scratch_shapes=[pltpu.VMEM((tq, 1), F32), pltpu.VMEM((tq, 1), F32),
                        pltpu.VMEM((tq, 2 * DK_A), F32), pltpu.VMEM((tq, 2 * DK_A), F32)],
        compiler_params=_cparams(("arbitrary",) * 3),
        name="attn_prompt",
    )(lamv, qb, kbt, vb, sg)


def _attn_sample_kernel(pt_ref, lam_ref, q_ref, *rest, pages, page, past, t_dec, n_heads, lam_init):
    k_refs = rest[:pages]
    v_refs = rest[pages:2 * pages]
    kn_ref, vn_ref, sg_ref, o_ref, m_ref, acc_ref = rest[2 * pages:]
    j = pl.program_id(1)
    nrow = 2 * n_heads * t_dec
    ncol = page * n_heads

    @pl.when(j == 0)
    def _():
        m_ref[...] = jnp.full(m_ref.shape, NEG_BIG, F32)
        acc_ref[...] = jnp.zeros(acc_ref.shape, F32)

    q = q_ref[...]
    r = lax.broadcasted_iota(jnp.int32, (nrow, 1), 0)
    hrow = (r % (n_heads * t_dec)) // t_dec
    trow = r % t_dec
    slope2 = _alibi_slope(hrow, n_heads) * LOG2E
    col = lax.broadcasted_iota(jnp.int32, (1, ncol), 1)
    tok = col // n_heads
    same_head = (col % n_heads) == hrow
    base = jnp.where(same_head, slope2 * tok.astype(F32), NEG_BIG)

    for p in range(pages):
        off = slope2 * ((j * pages + p) * page - past).astype(F32)
        s = _dot_nt(q, k_refs[p][...].astype(BF16)) + (base + off)
        _softmax_step(s, _with_ones(v_refs[p][...].astype(BF16)), m_ref, acc_ref)

    @pl.when(j == pl.num_programs(1) - 1)
    def _():
        nnew = kn_ref.shape[0]
        ok = (tok[:, :nnew] <= trow) & (col[:, :nnew] < n_heads * t_dec)
        s = jnp.where(ok, _dot_nt(q, kn_ref[...]) + base[:, :nnew], NEG_BIG)
        _softmax_step(s, _with_ones(vn_ref[...]), m_ref, acc_ref)
        lam = _lambda_from(lam_ref, lam_init)
        acc = acc_ref[...]
        o = acc[:, :DK_A] / acc[:, DK_A:DK_A + 1]
        half = nrow // 2
        o = o[0:half] - lam * o[half:nrow]
        o = o * lax.rsqrt(jnp.mean(o * o, axis=-1, keepdims=True) + EPS)
        o_ref[...] = o * sg_ref[...] * (1.0 - lam_init)


def _attn_sample(pt_flat, lamv, q_rows, cache_kr, cache_vr, kn, vn, sg, n_pages, page, t_dec, past,
                 lam_init):
    bs, nrow, _ = q_rows.shape
    n_heads = nrow // (2 * t_dec)
    prow = page * n_heads
    pages = _largest_tile(n_pages, 16, 1)
    steps = n_pages // pages

    def page_spec(p):
        return pl.BlockSpec((prow, DK_A), lambda b, j, pt: (pt[b * n_pages + j * pages + p], 0))

    per_b = lambda a: pl.BlockSpec((None,) + a.shape[1:], lambda b, j, pt: (b, 0, 0))
    const = lambda a: pl.BlockSpec(a.shape, lambda b, j, pt: (0, 0))
    grid_spec = pltpu.PrefetchScalarGridSpec(
        num_scalar_prefetch=1,
        grid=(bs, steps),
        in_specs=[const(lamv), per_b(q_rows)] + [page_spec(p) for p in range(pages)]
                 + [page_spec(p) for p in range(pages)] + [per_b(kn), per_b(vn), const(sg)],
        out_specs=pl.BlockSpec((None, nrow // 2, DK_A), lambda b, j, pt: (b, 0, 0)),
        scratch_shapes=[pltpu.VMEM((nrow, 1), F32), pltpu.VMEM((nrow, 2 * DK_A), F32)],
    )
    return pl.pallas_call(
        functools.partial(_attn_sample_kernel, pages=pages, page=page, past=past, t_dec=t_dec,
                          n_heads=n_heads, lam_init=lam_init),
        out_shape=jax.ShapeDtypeStruct((bs, nrow // 2, DK_A), F32),
        grid_spec=grid_spec,
        compiler_params=_cparams(("arbitrary", "arbitrary")),
        name="attn_sample",
    )(pt_flat, lamv, q_rows, *([cache_kr] * pages), *([cache_vr] * pages), kn, vn, sg)


CONV_B_HALO = 32


def _convb_kernel(u_ref, buf_ref, w_ref, b_ref, lg_ref, lb_ref, cv_ref, nb_ref, xp_ref, *, tm):
    t = pl.program_id(1)
    pad = CONV_B_HALO - (CONV_B_WIDTH - 1)

    @pl.when(t == 0)
    def _():
        xp_ref[0:CONV_B_HALO] = buf_ref[...]

    @pl.when(t > 0)
    def _():
        xp_ref[0:CONV_B_HALO] = xp_ref[tm:tm + CONV_B_HALO]

    xp_ref[CONV_B_HALO:CONV_B_HALO + tm] = u_ref[...]
    ch = min(64, tm)
    for c in range(tm // ch):
        acc = jnp.zeros((ch, u_ref.shape[1]), F32)
        for j in range(CONV_B_WIDTH):
            s0 = c * ch + pad + j
            acc = acc + w_ref[j:j + 1, :] * xp_ref[s0:s0 + ch, :]
        y = acc + b_ref[...]
        mu = jnp.mean(y, axis=-1, keepdims=True)
        yc = y - mu
        z = yc * lax.rsqrt(jnp.mean(yc * yc, axis=-1, keepdims=True) + EPS) * lg_ref[...] + lb_ref[...]
        cv_ref[c * ch:(c + 1) * ch] = (z * jax.nn.sigmoid(z)).astype(cv_ref.dtype)
    nb_ref[...] = xp_ref[tm:tm + CONV_B_HALO]


def _convb(u3, buf_pad, w_pad, b, lg, lb, out_dtype):
    nb, t, c = u3.shape
    tm = t if t < SUBLANES else _largest_tile(t, 256)
    nt = t // tm
    const = lambda a: pl.BlockSpec(a.shape, lambda bb, tt: (0, 0))
    return pl.pallas_call(
        functools.partial(_convb_kernel, tm=tm),
        out_shape=[jax.ShapeDtypeStruct((nb, t, c), out_dtype),
                   jax.ShapeDtypeStruct((nb, CONV_B_HALO, c), F32)],
        grid=(nb, nt),
        in_specs=[pl.BlockSpec((None, tm, c), lambda bb, tt: (bb, tt, 0)),
                  pl.BlockSpec((None, CONV_B_HALO, c), lambda bb, tt: (bb, 0, 0)),
                  const(w_pad), const(b), const(lg), const(lb)],
        out_specs=[pl.BlockSpec((None, tm, c), lambda bb, tt: (bb, tt, 0)),
                   pl.BlockSpec((None, CONV_B_HALO, c), lambda bb, tt: (bb, 0, 0))],
        scratch_shapes=[pltpu.VMEM((tm + CONV_B_HALO, c), F32)],
        compiler_params=_cparams(("arbitrary", "arbitrary")),
        name="conv_b",
    )(u3, buf_pad, w_pad, b, lg, lb)


def _outproj_even_kernel(o_ref, cv_ref, w_ref, x_ref, g1_ref, gf_ref, sc2_ref, sh2_ref,
                         x1_ref, h2_ref, *, aw):
    mix = (_dot(o_ref[...].astype(BF16), w_ref[0:aw, :])
           + _dot(cv_ref[...].astype(BF16), w_ref[aw:, :]))
    x1 = x_ref[...] + g1_ref[...] * mix
    x1_ref[...] = x1
    h2_ref[...] = _modnorm(x1, gf_ref[...], sc2_ref[...], sh2_ref[...])


def _outproj_even(o, cv, w_bf, x, tm, g1, gf, sc2, sh2, mod_map):
    rows, aw = o.shape
    d = x.shape[1]
    mr = g1.shape[1]
    const = lambda a: pl.BlockSpec(a.shape, lambda i: (0,) * a.ndim)
    mod_spec = pl.BlockSpec((None, mr, d), lambda i: (mod_map(i), 0, 0))
    xrow = pl.BlockSpec((tm, d), lambda i: (i, 0))
    return pl.pallas_call(
        functools.partial(_outproj_even_kernel, aw=aw),
        out_shape=[jax.ShapeDtypeStruct((rows, d), F32)] * 2,
        grid=(rows // tm,),
        in_specs=[pl.BlockSpec((tm, aw), lambda i: (i, 0)),
                  pl.BlockSpec((tm, cv.shape[1]), lambda i: (i, 0)),
                  const(w_bf), xrow, mod_spec, const(gf), mod_spec, mod_spec],
        out_specs=[xrow, xrow],
        compiler_params=_cparams(("arbitrary",)),
        name="outproj_even",
    )(o, cv, w_bf, x, g1, gf, sc2, sh2)


CONV_C_HALO = 8


def _odd_kernel(x_ref, gm_ref, sc1_ref, sh1_ref, win_ref, cw_ref, halo_ref, wout_ref, g1_ref,
                gf_ref, sc2_ref, sh2_ref, x1_ref, h2_ref, nb_ref, zp_ref,
                *, tm, cwid, seq_tiled, t_dec):
    x = x_ref[...]
    h = _modnorm(x, gm_ref[...], sc1_ref[...], sh1_ref[...]).astype(BF16)
    cg = _dot(h, win_ref[:, cwid:2 * cwid])
    hx = _dot(h, win_ref[:, 2 * cwid:3 * cwid])
    z = cg * hx
    halo = CONV_C_HALO
    if seq_tiled:
        t = pl.program_id(1)

        @pl.when(t == 0)
        def _():
            zp_ref[0:halo] = halo_ref[...]

        @pl.when(t > 0)
        def _():
            zp_ref[0:halo] = zp_ref[tm:tm + halo]
    else:
        zp_ref[0:halo] = jnp.zeros((halo, cwid), F32)
    zp_ref[halo:halo + tm] = z
    prev1 = zp_ref[halo - 1:halo - 1 + tm]
    prev2 = zp_ref[halo - 2:halo - 2 + tm]
    if seq_tiled:
        nb_ref[...] = zp_ref[tm:tm + halo]
    else:
        tpos = lax.broadcasted_iota(jnp.int32, (tm, 1), 0) % t_dec
        prev1 = jnp.where(tpos >= 1, prev1, halo_ref[0])
        prev2 = jnp.where(tpos >= 2, prev2, halo_ref[1])
        nb_ref[...] = z
    y = cw_ref[0:1, :] * prev2 + cw_ref[1:2, :] * prev1 + cw_ref[2:3, :] * z
    bg = _dot(h, win_ref[:, 0:cwid])
    mix = _dot((bg * y).astype(BF16), wout_ref[...])
    x1 = x + g1_ref[...] * mix
    x1_ref[...] = x1
    h2_ref[...] = _modnorm(x1, gf_ref[...], sc2_ref[...], sh2_ref[...])


def _odd_mixer(x, nb, t, tm, gm, sc1, sh1, mod_map, win_bf, cw, halo, wout_bf, g1,
               gf, sc2, sh2, seq_tiled, t_dec):
    rows, d = x.shape
    cwid = wout_bf.shape[0]
    mr = g1.shape[1]
    if seq_tiled:
        nt = t // tm
        grid = (nb, nt)
        xmap = lambda b, i: (b * nt + i, 0)
        halo_spec = pl.BlockSpec((None, CONV_C_HALO, cwid), lambda b, i: (b, 0, 0))
        nb_shape = jax.ShapeDtypeStruct((nb, CONV_C_HALO, cwid), F32)
        nb_spec = pl.BlockSpec((None, CONV_C_HALO, cwid), lambda b, i: (b, 0, 0))
        mmap = lambda b, i: (mod_map(b, i), 0, 0)
    else:
        grid = (1, 1)
        xmap = lambda b, i: (0, 0)
        halo_spec = pl.BlockSpec(halo.shape, lambda b, i: (0, 0, 0))
        nb_shape = jax.ShapeDtypeStruct((tm, cwid), F32)
        nb_spec = pl.BlockSpec((tm, cwid), lambda b, i: (0, 0))
        mmap = lambda b, i: (0, 0, 0)
    const = lambda a: pl.BlockSpec(a.shape, lambda b, i: (0,) * a.ndim)
    mod_spec = pl.BlockSpec((None, mr, d), mmap)
    xrow = pl.BlockSpec((tm, d), xmap)
    return pl.pallas_call(
        functools.partial(_odd_kernel, tm=tm, cwid=cwid, seq_tiled=seq_tiled, t_dec=t_dec),
        out_shape=[jax.ShapeDtypeStruct((rows, d), F32)] * 2 + [nb_shape],
        grid=grid,
        in_specs=[xrow, const(gm), mod_spec, mod_spec, const(win_bf), const(cw), halo_spec,
                  const(wout_bf), mod_spec, const(gf), mod_spec, mod_spec],
        out_specs=[xrow, xrow, nb_spec],
        scratch_shapes=[pltpu.VMEM((tm + CONV_C_HALO, cwid), F32)],
        compiler_params=_cparams(("arbitrary", "arbitrary")),
        name="odd_mixer",
    )(x, gm, sc1, sh1, win_bf, cw, halo, wout_bf, g1, gf, sc2, sh2)


def _route_kernel(h_ref, w0_ref, w1_ref, w2_ref, b_ref, tri_ref, cnt0_ref, meta_ref, cnt_ref, carry_ref):
    i = pl.program_id(0)

    @pl.when(i == 0)
    def _():
        carry_ref[...] = cnt0_ref[...]

    a0, a1, a2 = _split3(h_ref[...])
    w0, w1, w2 = w0_ref[...], w1_ref[...], w2_ref[...]
    logits = (_dot(a0, w0) + (_dot(a0, w1) + _dot(a1, w0))
              + (_dot(a1, w1) + _dot(a0, w2) + _dot(a2, w0))) + b_ref[...]
    tm = logits.shape[0]
    lane = lax.broadcasted_iota(jnp.int32, logits.shape, 1)
    lane_f = lane.astype(F32)
    neg = jnp.float32(-jnp.inf)

    def first_max(vals):
        vmax = jnp.max(vals, axis=-1, keepdims=True)
        idx = jnp.min(jnp.where(vals == vmax, lane_f, float(LANES)), axis=-1, keepdims=True)
        return vmax, idx.astype(jnp.int32)

    gmask = lane < N_GROUPS
    gmax, g_idx = first_max(jnp.where(gmask, logits, neg))
    g_w = 1.0 / jnp.sum(jnp.where(gmask, jnp.exp(logits - gmax), 0.0), axis=-1, keepdims=True)
    lo = N_GROUPS + EXPERTS_PER_GROUP * g_idx
    el = jnp.where((lane >= lo) & (lane < lo + EXPERTS_PER_GROUP), logits, neg)
    v1, i1 = first_max(el)
    v2, i2 = first_max(jnp.where(lane == i1, neg, el))
    tt = jnp.exp(v2 - v1)
    wgt1 = g_w / (1.0 + tt)
    wgt2 = g_w * tt / (1.0 + tt)
    e1 = i1 - N_GROUPS
    e2 = i2 - N_GROUPS
    hit1 = lane == e1
    hit2 = lane == e2
    onehot = (hit1 | hit2).astype(F32)
    cnt = _dot(tri_ref[...], onehot.astype(BF16)) + carry_ref[...]
    r1 = jnp.sum(jnp.where(hit1, cnt, 0.0), axis=-1, keepdims=True)
    r2 = jnp.sum(jnp.where(hit2, cnt, 0.0), axis=-1, keepdims=True)
    carry_ref[...] = carry_ref[...] + jnp.sum(onehot, axis=0, keepdims=True)
    cnt_ref[...] = carry_ref[...]
    cols = (e1.astype(F32), e2.astype(F32), r1, r2, wgt1, wgt2)
    meta = jnp.zeros((tm, LANES), F32)
    for c, val in enumerate(cols):
        meta = jnp.where(lane == c, val, meta)
    meta_ref[...] = meta


def _route(h2, w_parts, b_pad, cnt0, tm):
    n, d = h2.shape
    tri = (jnp.arange(tm)[:, None] > jnp.arange(tm)[None, :]).astype(BF16)
    const = lambda a: pl.BlockSpec(a.shape, lambda i: (0, 0))
    return pl.pallas_call(
        _route_kernel,
        out_shape=[jax.ShapeDtypeStruct((n, LANES), F32), jax.ShapeDtypeStruct((1, LANES), F32)],
        grid=(n // tm,),
        in_specs=[pl.BlockSpec((tm, d), lambda i: (i, 0)), const(w_parts[0]), const(w_parts[1]),
                  const(w_parts[2]), const(b_pad), const(tri), const(cnt0)],
        out_specs=[pl.BlockSpec((tm, LANES), lambda i: (i, 0)), const(cnt0)],
        scratch_shapes=[pltpu.VMEM((1, LANES), F32)],
        compiler_params=_cparams(("arbitrary",)),
        name="moe_route",
    )(h2, *w_parts, b_pad, tri, cnt0)


def _dispatch_kernel(dest_ref, h_ref, xp_in_ref, xp_ref, sem, *, tm):
    del xp_in_ref

    def row_copy(r, slot):
        return pltpu.make_async_copy(h_ref.at[pl.ds(r, 1)], xp_ref.at[pl.ds(slot, 1)], sem)

    def issue(r, c):
        for k in range(TOP_K):
            row_copy(r, dest_ref[0, TOP_K * r + k]).start()
        return c

    lax.fori_loop(0, tm, issue, 0, unroll=DMA_UNROLL)

    def drain(r, c):
        for k in range(TOP_K):
            row_copy(r, dest_ref[0, TOP_K * r + k]).wait()
        return c

    lax.fori_loop(0, tm, drain, 0, unroll=DMA_UNROLL)


def _dispatch(dest3, h2_all, xp_zero, tm):
    n, d = h2_all.shape
    return pl.pallas_call(
        functools.partial(_dispatch_kernel, tm=tm),
        out_shape=jax.ShapeDtypeStruct(xp_zero.shape, F32),
        grid=(n // tm,),
        in_specs=[pl.BlockSpec((None, 1, TOP_K * tm), lambda i: (i, 0, 0), memory_space=pltpu.SMEM),
                  pl.BlockSpec((tm, d), lambda i: (i, 0)),
                  pl.BlockSpec(memory_space=pl.ANY)],
        out_specs=pl.BlockSpec(memory_space=pl.ANY),
        input_output_aliases={2: 0},
        scratch_shapes=[pltpu.SemaphoreType.DMA],
        compiler_params=_cparams(("arbitrary",)),
        name="moe_dispatch",
    )(dest3, h2_all, xp_zero)


def _expert_kernel(be_ref, nu_ref, x_ref, wgu_ref, wd_ref, y_ref, wgu_bf, wd_bf, *, f):
    b = pl.program_id(0)
    prev = be_ref[jnp.maximum(b - 1, 0)]

    @pl.when((b == 0) | (be_ref[b] != prev))
    def _():
        wgu_bf[...] = wgu_ref[...].astype(BF16)
        wd_bf[...] = wd_ref[...].astype(BF16)

    @pl.when(b < nu_ref[0])
    def _():
        x = x_ref[...].astype(BF16)
        gate = _dot(x, wgu_bf[:, 0:f])
        up = _dot(x, wgu_bf[:, f:2 * f])
        act = (gate * jax.nn.sigmoid(gate) * up).astype(BF16)
        y_ref[...] = _dot(act, wd_bf[...])

    @pl.when(b >= nu_ref[0])
    def _():
        y_ref[...] = jnp.zeros(y_ref.shape, F32)


def _experts(block_e, n_used, xp, w_gate_up, w_down, layer):
    n_slots, d = xp.shape
    f = w_down.shape[2]
    grid_spec = pltpu.PrefetchScalarGridSpec(
        num_scalar_prefetch=2,
        grid=(n_slots // MOE_ROWS,),
        in_specs=[pl.BlockSpec((MOE_ROWS, d), lambda b, be, nu: (b, 0)),
                  pl.BlockSpec((None, None, d, 2 * f), lambda b, be, nu: (layer, be[b], 0, 0)),
                  pl.BlockSpec((None, None, f, d), lambda b, be, nu: (layer, be[b], 0, 0))],
        out_specs=pl.BlockSpec((MOE_ROWS, d), lambda b, be, nu: (b, 0)),
        scratch_shapes=[pltpu.VMEM((d, 2 * f), BF16), pltpu.VMEM((f, d), BF16)],
    )
    return pl.pallas_call(
        functools.partial(_expert_kernel, f=f),
        out_shape=jax.ShapeDtypeStruct((n_slots, d), F32),
        grid_spec=grid_spec,
        compiler_params=_cparams(("arbitrary",)),
        name="moe_experts",
    )(block_e, n_used, xp, w_gate_up, w_down)


def _combine_kernel(dest_ref, x_ref, meta_ref, g2_ref, yp_ref, o_ref, ybuf, sem, *, tm):

    def row_copy(r, k, slot):
        return pltpu.make_async_copy(yp_ref.at[pl.ds(slot, 1)], ybuf.at[k, pl.ds(r, 1)], sem)

    def issue(r, c):
        for k in range(TOP_K):
            row_copy(r, k, dest_ref[0, TOP_K * r + k]).start()
        return c

    lax.fori_loop(0, tm, issue, 0, unroll=DMA_UNROLL)

    def drain(r, c):
        for k in range(TOP_K):
            row_copy(r, k, dest_ref[0, TOP_K * r + k]).wait()
        return c

    lax.fori_loop(0, tm, drain, 0, unroll=DMA_UNROLL)
    meta = meta_ref[...]
    y = meta[:, 4:5] * ybuf[0] + meta[:, 5:6] * ybuf[1]
    o_ref[...] = x_ref[...] + g2_ref[...] * y


def _combine(dest3, x1, meta, g2, mod_map, yp, tm):
    rows, d = x1.shape
    mr = g2.shape[1]
    xrow = pl.BlockSpec((tm, d), lambda i: (i, 0))
    return pl.pallas_call(
        functools.partial(_combine_kernel, tm=tm),
        out_shape=jax.ShapeDtypeStruct((rows, d), F32),
        grid=(rows // tm,),
        in_specs=[pl.BlockSpec((None, 1, TOP_K * tm), lambda i: (i, 0, 0), memory_space=pltpu.SMEM),
                  xrow, pl.BlockSpec((tm, LANES), lambda i: (i, 0)),
                  pl.BlockSpec((None, mr, d), lambda i: (mod_map(i), 0, 0)),
                  pl.BlockSpec(memory_space=pl.ANY)],
        out_specs=xrow,
        scratch_shapes=[pltpu.VMEM((TOP_K, tm, d), F32), pltpu.SemaphoreType.DMA],
        compiler_params=_cparams(("arbitrary",)),
        name="moe_combine",
    )(dest3, x1, meta, g2, yp)


def _moe(h2_p, h2_s, x1_p, x1_s, g2_p, g2_s, w_parts, b_pad, w_gate_up, w_down, layer, tm_p, seq):
    n_p, d = h2_p.shape
    n_s = h2_s.shape[0]
    n = n_p + n_s
    meta_p, cnt_p = _route(h2_p, w_parts, b_pad, jnp.zeros((1, LANES), F32), tm_p)
    meta_s, counts = _route(h2_s, w_parts, b_pad, cnt_p, n_s)
    sizes = counts[0, :N_EXPERTS].astype(jnp.int32)
    padded = ((sizes + MOE_ROWS - 1) // MOE_ROWS) * MOE_ROWS
    pad_end = jnp.cumsum(padded)
    pad_start = pad_end - padded

    def slots(meta):
        e_tok = meta[:, 0:TOP_K].astype(jnp.int32)
        rank = meta[:, TOP_K:2 * TOP_K].astype(jnp.int32)
        start = jnp.sum(jnp.where(e_tok[:, :, None] == jnp.arange(N_EXPERTS)[None, None, :],
                                  pad_start[None, None, :], 0), axis=-1)
        return (start + rank).astype(jnp.int32)

    dest_p = slots(meta_p).reshape(n_p // tm_p, 1, TOP_K * tm_p)
    dest_s = slots(meta_s).reshape(1, 1, TOP_K * n_s)
    n_blocks = (n * TOP_K + MOE_ROWS - 1) // MOE_ROWS + N_EXPERTS
    blk0 = jnp.arange(n_blocks, dtype=jnp.int32) * MOE_ROWS
    block_e = jnp.minimum(jnp.sum(pad_end[None, :] <= blk0[:, None], axis=-1), N_EXPERTS - 1).astype(jnp.int32)
    n_used = (pad_end[-1:] // MOE_ROWS).astype(jnp.int32)

    xp = _dispatch(dest_p, h2_p, jnp.zeros((n_blocks * MOE_ROWS, d), F32), tm_p)
    xp = _dispatch(dest_s, h2_s, xp, n_s)
    yp = _experts(block_e, n_used, xp, w_gate_up, w_down, layer)
    per_seq = seq // tm_p
    out_p = _combine(dest_p, x1_p, meta_p, g2_p, lambda i: i // per_seq, yp, tm_p)
    out_s = _combine(dest_s, x1_s, meta_s, g2_s, lambda i: 0, yp, n_s)
    return out_p, out_s


def kernel(x_prompt, x_sample, cache_k, cache_v, state_conv_b, state_conv_c, page_table, c_prompt, c_sample, norm_mix_g, norm_ffn_g, w_ada, b_ada, w_in_a, q_norm_g, k_norm_g, lam_q1, lam_k1, lam_q2, lam_k2, subln_g, convb_w, convb_b, convb_ln_g, convb_ln_b, w_out_a, w_in_c, convc_w, w_out_c, w_group, b_group, w_router, b_router, w_gate_up, w_down):
    bp, seq, d = x_prompt.shape
    bs, t_dec, _ = x_sample.shape
    depth = w_ada.shape[0]
    n_p, n_s = bp * seq, bs * t_dec
    n = n_p + n_s
    n_pool, page = cache_k.shape[1], cache_k.shape[2]
    n_heads = cache_k.shape[3]
    aw = n_heads * DK_A
    bw = state_conv_b.shape[-1]
    cwid = state_conv_c.shape[-1]
    n_pages = page_table.shape[1]
    past = n_pages * page
    tm_p = _largest_tile(seq, 512, LANES)
    assert n_s % SUBLANES == 0

    x_p = x_prompt.reshape(n_p, d)
    x_s = x_sample.reshape(n_s, d)

    c_all = jnp.concatenate([c_prompt, c_sample], axis=0)
    rows = c_all.shape[0]
    rows_pad = -(-rows // SUBLANES) * SUBLANES
    mods = _ada_all(jnp.pad(c_all, ((0, rows_pad - rows), (0, 0))), w_ada, b_ada)
    mods = mods[:, :rows].reshape(depth, rows, 6, d)

    def mod_rows(l, j):
        m_p = mods[l, :bp, j][:, None, :]
        m_s = jnp.repeat(mods[l, bp:, j], t_dec, axis=0)[None]
        return m_p, m_s

    bd = (jnp.arange(aw)[:, None] // HD_A == jnp.arange(aw)[None, :] // HD_A).astype(BF16)
    cache_kr = cache_k.reshape(-1, DK_A)
    cache_vr = cache_v.reshape(-1, DK_A)
    per_seq = seq // tm_p
    pmap1 = lambda i: i // per_seq
    smap1 = lambda i: 0
    row2 = lambda a: a.reshape(1, -1)

    k_p, v_p, k_s, v_s, cb_p, cb_s, cc_p, cc_s = [], [], [], [], [], [], [], []
    for l in range(depth):
        i = l // 2
        sh1_p, sh1_s = mod_rows(l, 0)
        sc1_p, sc1_s = mod_rows(l, 1)
        g1_p, g1_s = mod_rows(l, 2)
        sh2_p, sh2_s = mod_rows(l, 3)
        sc2_p, sc2_s = mod_rows(l, 4)
        g2_p, g2_s = mod_rows(l, 5)
        gm = row2(norm_mix_g[l])
        gf = row2(norm_ffn_g[l])
        if l % 2 == 0:
            lam_init = 0.8 - 0.6 * math.exp(-0.3 * l)
            w_in = w_in_a[i].astype(BF16)
            w_out = w_out_a[i].astype(BF16)
            qg = jnp.tile(q_norm_g[i].reshape(1, DK_A), (1, n_heads))
            kg = jnp.tile(k_norm_g[i].reshape(1, DK_A), (1, n_heads))
            lamv = jnp.stack([lam_q1[i], lam_k1[i], lam_q2[i], lam_k2[i]])
            sg = row2(subln_g[i])
            cw = jnp.pad(convb_w[i].reshape(CONV_B_WIDTH, bw), ((0, 1), (0, 0)))
            cbb, clg, clb = row2(convb_b[i]), row2(convb_ln_g[i]), row2(convb_ln_b[i])
            halo_pad = ((0, 0), (CONV_B_HALO - (CONV_B_WIDTH - 1), 0), (0, 0))

            qb, kf, kbt, vf, vb, u = _inproj_even(x_p, tm_p, gm, sc1_p, sh1_p, pmap1,
                                                  w_in, bd, qg, kg, aw, bw, True)
            o_p = _attn_prompt(lamv, qb, kbt, vb, sg, bp, seq, lam_init)
            cv_p, nb_p = _convb(u.reshape(bp, seq, bw), jnp.zeros((bp, CONV_B_HALO, bw), F32),
                                cw, cbb, clg, clb, BF16)
            x1_p, h2_p = _outproj_even(o_p, cv_p.reshape(n_p, bw), w_out, x_p, tm_p,
                                       g1_p, gf, sc2_p, sh2_p, pmap1)
            k_p.append(kf.reshape(bp, seq, n_heads, DK_A))
            v_p.append(vf.reshape(bp, seq, n_heads, DK_A))
            cb_p.append(nb_p[:, CONV_B_HALO - (CONV_B_WIDTH - 1):])

            qb, kf, kb, vf, vb, u = _inproj_even(x_s, n_s, gm, sc1_s, sh1_s, smap1,
                                                 w_in, bd, qg, kg, aw, bw, False)
            q4 = qb.reshape(bs, t_dec, n_heads, 2, HD_A).transpose(0, 2, 1, 3, 4)[:, None]
            eye_m = jnp.eye(2, dtype=BF16)[None, :, None, None, :, None]
            q_rows = (q4 * eye_m).reshape(bs, 2 * n_heads * t_dec, DK_A)
            new_pad = ((0, 0), (0, LANES - t_dec * n_heads), (0, 0))
            kn = jnp.pad(kb.reshape(bs, t_dec * n_heads, DK_A), new_pad)
            vn = jnp.pad(vb.reshape(bs, t_dec * n_heads, DK_A), new_pad)
            pt_flat = (page_table.astype(jnp.int32) + i * n_pool).reshape(-1)
            o_s = _attn_sample(pt_flat, lamv, q_rows, cache_kr, cache_vr, kn, vn, sg, n_pages, page,
                               t_dec, past, lam_init)
            o_s = o_s.reshape(bs, n_heads, t_dec, DK_A).transpose(0, 2, 1, 3)
            cv_s, nb_s = _convb(u.reshape(bs, t_dec, bw), jnp.pad(state_conv_b[i], halo_pad),
                                cw, cbb, clg, clb, F32)
            x1_s, h2_s = _outproj_even(o_s.reshape(n_s, aw), cv_s.reshape(n_s, bw), w_out,
                                       x_s, n_s, g1_s, gf, sc2_s, sh2_s, smap1)
            k_s.append(kf.reshape(bs, t_dec, n_heads, DK_A))
            v_s.append(vf.reshape(bs, t_dec, n_heads, DK_A))
            cb_s.append(nb_s[:, CONV_B_HALO - (CONV_B_WIDTH - 1):])
        else:
            w_in = w_in_c[i].astype(BF16)
            w_out = w_out_c[i].astype(BF16)
            cw = jnp.pad(convc_w[i].reshape(CONV_C_WIDTH, cwid), ((0, SUBLANES - CONV_C_WIDTH), (0, 0)))
            halo_p = jnp.zeros((bp, CONV_C_HALO, cwid), F32)
            x1_p, h2_p, nb_p = _odd_mixer(
                x_p, bp, seq, tm_p, gm, sc1_p, sh1_p, lambda b, t: b, w_in, cw, halo_p, w_out,
                g1_p, gf, sc2_p, sh2_p, True, t_dec)
            cc_p.append(nb_p[:, CONV_C_HALO - (CONV_C_WIDTH - 1):])
            st = state_conv_c[i]
            zrow = jnp.zeros((bs, 1, cwid), F32)
            halo1 = jnp.concatenate([st[:, 1:2]] + [zrow] * (t_dec - 1), axis=1).reshape(n_s, cwid)
            halo2 = jnp.concatenate([st[:, 0:1], st[:, 1:2]] + [zrow] * (t_dec - 2), axis=1).reshape(n_s, cwid)
            x1_s, h2_s, z_s = _odd_mixer(
                x_s, 1, n_s, n_s, gm, sc1_s, sh1_s, None, w_in, cw, jnp.stack([halo1, halo2]),
                w_out, g1_s, gf, sc2_s, sh2_s, False, t_dec)
            cc_s.append(z_s.reshape(bs, t_dec, cwid)[:, t_dec - (CONV_C_WIDTH - 1):])

        w_r = jnp.pad(jnp.concatenate([w_group[l], w_router[l]], axis=1),
                      ((0, 0), (0, LANES - N_GROUPS - N_EXPERTS)))
        w0 = w_r.astype(BF16)
        rem = w_r - w0.astype(F32)
        w1 = rem.astype(BF16)
        w2 = (rem - w1.astype(F32)).astype(BF16)
        b_r = jnp.pad(jnp.concatenate([b_group[l], b_router[l]]), (0, LANES - N_GROUPS - N_EXPERTS))
        x_p, x_s = _moe(h2_p, h2_s, x1_p, x1_s, g2_p, g2_s, (w0, w1, w2), row2(b_r),
                        w_gate_up, w_down, l, tm_p, seq)

    return (x_p.reshape(bp, seq, d), x_s.reshape(bs, t_dec, d),
            jnp.stack(k_p), jnp.stack(v_p), jnp.stack(k_s), jnp.stack(v_s),
            jnp.stack(cb_p), jnp.stack(cb_s), jnp.stack(cc_p), jnp.stack(cc_s))
```

```python
import functools
import math

import jax
import jax.numpy as jnp
from jax import lax
from jax.experimental import pallas as pl
from jax.experimental.pallas import tpu as pltpu

F32 = jnp.float32
BF16 = jnp.bfloat16

HD_A = 64
DK_A = 2 * HD_A
N_GROUPS = 4
EXPERTS_PER_GROUP = 8
N_EXPERTS = N_GROUPS * EXPERTS_PER_GROUP
TOP_K = 2
CONV_B_WIDTH = 31
CONV_C_WIDTH = 3
ALIBI_MAX = 8.0
EPS = 1e-6
NEG_BIG = -1e30
LOG2E = math.log2(math.e)

LANES = 128
SUBLANES = 8
MOE_ROWS = 256
STAB_LIMIT = 60.0
BOUND_SLACK = 1.01
ATTN_TILE = 512
DMA_UNROLL = 8
VMEM_LIMIT = 56 * 1024 * 1024


def _cparams(sem, vmem=VMEM_LIMIT):
    return pltpu.CompilerParams(dimension_semantics=sem, vmem_limit_bytes=vmem)


def _largest_tile(n, cap, mult=SUBLANES):
    best = None
    for t in range(mult, min(n, cap) + 1, mult):
        if n % t == 0:
            best = t
    assert best is not None, (n, cap, mult)
    return best


def _modnorm(x, g, scale, shift):
    ms = jnp.mean(x * x, axis=-1, keepdims=True)
    return (x * lax.rsqrt(ms + EPS)) * g * (1.0 + scale) + shift


def _dot(a, b):
    return jnp.dot(a, b, preferred_element_type=F32)


def _dot_nt(a, b):
    return lax.dot_general(a, b, (((1,), (1,)), ((), ())), preferred_element_type=F32)


def _split3(a):
    a0 = a.astype(BF16)
    r1 = a - a0.astype(F32)
    a1 = r1.astype(BF16)
    a2 = (r1 - a1.astype(F32)).astype(BF16)
    return a0, a1, a2


def _ada_kernel(c_ref, w_ref, b_ref, o_ref):
    c = c_ref[...]
    a = (c * jax.nn.sigmoid(c)).astype(BF16)
    o_ref[...] = _dot(a, w_ref[...].astype(BF16)) + b_ref[...]


def _ada_all(c_pad, w_ada, b_ada):
    depth, d, n6 = w_ada.shape
    rows = c_pad.shape[0]
    tn = 1536
    return pl.pallas_call(
        _ada_kernel,
        out_shape=jax.ShapeDtypeStruct((depth, rows, n6), F32),
        grid=(depth, n6 // tn),
        in_specs=[pl.BlockSpec((rows, d), lambda l, j: (0, 0)),
                  pl.BlockSpec((None, d, tn), lambda l, j: (l, 0, j)),
                  pl.BlockSpec((None, 1, tn), lambda l, j: (l, 0, j))],
        out_specs=pl.BlockSpec((None, rows, tn), lambda l, j: (l, 0, j)),
        compiler_params=_cparams(("arbitrary", "arbitrary")),
        name="ada_mod",
    )(c_pad, w_ada, b_ada.reshape(depth, 1, n6))


def _inproj_even_kernel(x_ref, g_ref, sc_ref, sh_ref, w_ref, bd_ref, qg_ref, kg_ref,
                        qb_ref, kf_ref, kb_ref, vf_ref, vb_ref, u_ref, *, aw, bw, k_transposed):
    h = _modnorm(x_ref[...], g_ref[...], sc_ref[...], sh_ref[...]).astype(BF16)
    bd = bd_ref[...]

    def group_norm(t, gain):
        tt = t * t
        hi = tt.astype(BF16)
        lo = (tt - hi.astype(F32)).astype(BF16)
        ss = _dot(hi, bd) + _dot(lo, bd)
        return t * lax.rsqrt(ss * (1.0 / HD_A) + EPS) * gain

    q = group_norm(_dot(h, w_ref[:, 0:aw]), qg_ref[...])
    qb_ref[...] = (q * (HD_A ** -0.5 * LOG2E)).astype(BF16)
    k = group_norm(_dot(h, w_ref[:, aw:2 * aw]), kg_ref[...])
    kf_ref[...] = k
    kb_ref[...] = (k.T if k_transposed else k).astype(BF16)
    v = _dot(h, w_ref[:, 2 * aw:3 * aw])
    vf_ref[...] = v
    vb_ref[...] = v.astype(BF16)
    ga = _dot(h, w_ref[:, 3 * aw:3 * aw + bw])
    gb = _dot(h, w_ref[:, 3 * aw + bw:3 * aw + 2 * bw])
    u_ref[...] = ga * jax.nn.sigmoid(gb)


def _inproj_even(x, tm, g, sc, sh, mod_map, w_bf, bd, qg, kg, aw, bw, k_transposed):
    rows, d = x.shape
    nt = rows // tm
    mr = sc.shape[1]
    row_spec = lambda width: pl.BlockSpec((tm, width), lambda i: (i, 0))
    const = lambda a: pl.BlockSpec(a.shape, lambda i: (0,) * a.ndim)
    mod_spec = pl.BlockSpec((None, mr, d), lambda i: (mod_map(i), 0, 0))
    kb_shape = (aw, rows) if k_transposed else (rows, aw)
    kb_spec = pl.BlockSpec((aw, tm), lambda i: (0, i)) if k_transposed else row_spec(aw)
    outs = [jax.ShapeDtypeStruct((rows, aw), BF16), jax.ShapeDtypeStruct((rows, aw), F32),
            jax.ShapeDtypeStruct(kb_shape, BF16), jax.ShapeDtypeStruct((rows, aw), F32),
            jax.ShapeDtypeStruct((rows, aw), BF16), jax.ShapeDtypeStruct((rows, bw), F32)]
    return pl.pallas_call(
        functools.partial(_inproj_even_kernel, aw=aw, bw=bw, k_transposed=k_transposed),
        out_shape=outs,
        grid=(nt,),
        in_specs=[pl.BlockSpec((tm, d), lambda i: (i, 0)), const(g), mod_spec, mod_spec,
                  const(w_bf), const(bd), const(qg), const(kg)],
        out_specs=[row_spec(aw), row_spec(aw), kb_spec, row_spec(aw), row_spec(aw), row_spec(bw)],
        compiler_params=_cparams(("arbitrary",)),
        name="inproj_even",
    )(x, g, sc, sh, w_bf, bd, qg, kg)


def _lambda_from(lam_ref, lam_init):
    lv = lam_ref[...]
    a = jnp.sum(lv[0:1] * lv[1:2], axis=-1, keepdims=True)
    b = jnp.sum(lv[2:3] * lv[3:4], axis=-1, keepdims=True)
    return jnp.exp(a) - jnp.exp(b) + lam_init


def _alibi_slope(head, n_heads):
    slope = jnp.float32(0.0)
    for hh in range(n_heads):
        slope = jnp.where(head == hh, jnp.float32(2.0 ** (-ALIBI_MAX * (hh + 1) / n_heads)), slope)
    return slope


def _softmax_step(s, v_ones, m_ref, acc_ref):
    m_old = m_ref[...]
    m_new = jnp.maximum(m_old, jnp.max(s, axis=-1, keepdims=True))
    p = jnp.exp2(s - m_new)
    acc_ref[...] = jnp.exp2(m_old - m_new) * acc_ref[...] + _dot(p.astype(BF16), v_ones)
    m_ref[...] = m_new


def _with_ones(v):
    return jnp.concatenate([v, jnp.ones(v.shape, v.dtype)], axis=-1)


def _attn_prompt_kernel(lam_ref, q_ref, kt_ref, v_ref, sg_ref, o_ref, kmax_ref, m1_ref, m2_ref,
                        acc1_ref, acc2_ref, *, tq, lam_init, n_heads):
    h = pl.program_id(1)
    qi = pl.program_id(2)

    @pl.when(qi == 0)
    def _():
        kk = kt_ref[...].astype(F32)
        kk = kk * kk
        for m in range(2):
            n2 = jnp.sum(kk[m * HD_A:(m + 1) * HD_A], axis=0, keepdims=True)
            kmax_ref[m] = jnp.sqrt(jnp.max(n2, axis=-1, keepdims=True))

    for acc_ref in (acc1_ref, acc2_ref):
        acc_ref[...] = jnp.zeros(acc_ref.shape, F32)
    slope2 = _alibi_slope(h, n_heads) * LOG2E
    q = q_ref[...]
    lane = lax.broadcasted_iota(jnp.int32, q.shape, 1)
    zero = jnp.zeros_like(q)
    q1 = jnp.where(lane < HD_A, q, zero)
    q2 = jnp.where(lane >= HD_A, q, zero)
    col = lax.broadcasted_iota(jnp.int32, (1, tq), 1)
    bound = []
    for m, qm in enumerate((q1, q2)):
        qf = qm.astype(F32)
        qn = jnp.sqrt(jnp.sum(qf * qf, axis=-1, keepdims=True))
        bound.append(qn * kmax_ref[m] * BOUND_SLACK)
    bounded = jnp.max(jnp.maximum(bound[0], bound[1])) < STAB_LIMIT

    def tiles(kj):
        r0 = pl.multiple_of(kj * tq, tq)
        colb = slope2 * ((kj - qi) * tq + col).astype(F32)
        return kt_ref[:, pl.ds(r0, tq)], _with_ones(v_ref[pl.ds(r0, tq), :]), colb

    def causal_keep():
        return (lax.broadcasted_iota(jnp.int32, (tq, tq), 0)
                >= lax.broadcasted_iota(jnp.int32, (tq, tq), 1))

    def run(step):
        def body(kj, carry):
            step(kj, False)
            return carry

        lax.fori_loop(0, qi, body, 0)
        step(qi, True)

    @pl.when(bounded)
    def _():
        row = lax.broadcasted_iota(jnp.int32, (tq, 1), 0).astype(F32)
        lane2 = lax.broadcasted_iota(jnp.int32, (tq, LANES), 1)
        sub = lax.broadcasted_iota(jnp.int32, (2 * SUBLANES, tq), 0)
        q_aug = []
        for qm, bm in ((q1, bound[0]), (q2, bound[1])):
            rt = -(bm + slope2 * row)
            rt_hi = rt.astype(BF16).astype(F32)
            rt_lo = (rt - rt_hi).astype(BF16).astype(F32)
            ext = jnp.where(lane2 == 0, rt_hi,
                            jnp.where(lane2 == 1, rt_lo, jnp.where(lane2 < 5, 1.0, 0.0)))
            q_aug.append(jnp.concatenate([qm, ext.astype(BF16)], axis=-1))
        k_pad = jnp.zeros((LANES - 2 * SUBLANES, tq), BF16)

        def step(kj, masked):
            kt, v_ones, colb = tiles(kj)
            c0, c1, c2 = (c.astype(F32) for c in _split3(colb))
            ext_k = jnp.where(sub < 2, 1.0,
                              jnp.where(sub == 2, c0,
                                        jnp.where(sub == 3, c1, jnp.where(sub == 4, c2, 0.0))))
            kt_aug = jnp.concatenate([kt, ext_k.astype(BF16), k_pad], axis=0)
            for qa, acc_ref in ((q_aug[0], acc1_ref), (q_aug[1], acc2_ref)):
                e = _dot(qa, kt_aug)
                if masked:
                    e = jnp.where(causal_keep(), e, NEG_BIG)
                acc_ref[...] += _dot(jnp.exp2(e).astype(BF16), v_ones)

        run(step)

    @pl.when(jnp.logical_not(bounded))
    def _():
        for m_ref in (m1_ref, m2_ref):
            m_ref[...] = jnp.full(m_ref.shape, NEG_BIG, F32)

        def step(kj, masked):
            kt, v_ones, colb = tiles(kj)
            for qm, m_ref, acc_ref in ((q1, m1_ref, acc1_ref), (q2, m2_ref, acc2_ref)):
                s = _dot(qm, kt) + colb
                if masked:
                    s = jnp.where(causal_keep(), s, NEG_BIG)
                _softmax_step(s, v_ones, m_ref, acc_ref)

        run(step)

    lam = _lambda_from(lam_ref, lam_init)
    a1 = acc1_ref[...]
    a2 = acc2_ref[...]
    o = a1[:, :DK_A] / a1[:, DK_A:DK_A + 1] - lam * (a2[:, :DK_A] / a2[:, DK_A:DK_A + 1])
    o = o * lax.rsqrt(jnp.mean(o * o, axis=-1, keepdims=True) + EPS)
    o_ref[...] = (o * sg_ref[...] * (1.0 - lam_init)).astype(o_ref.dtype)


def _attn_prompt(lamv, qb, kbt, vb, sg, batch, seq, lam_init):
    n_heads = qb.shape[1] // DK_A
    tq = _largest_tile(seq, ATTN_TILE, LANES)
    nq = seq // tq
    q_spec = pl.BlockSpec((tq, DK_A), lambda b, h, i: (b * nq + i, h))
    kt_spec = pl.BlockSpec((DK_A, seq), lambda b, h, i: (h, b))
    kv_spec = pl.BlockSpec((seq, DK_A), lambda b, h, i: (b, h))
    return pl.pallas_call(
        functools.partial(_attn_prompt_kernel, tq=tq, lam_init=lam_init, n_heads=n_heads),
        out_shape=jax.ShapeDtypeStruct(qb.shape, BF16),
        grid=(batch, n_heads, nq),
        in_specs=[pl.BlockSpec(lamv.shape, lambda b, h, i: (0, 0)), q_spec, kt_spec, kv_spec,
                  pl.BlockSpec(sg.shape, lambda b, h, i: (0, 0))],
        out_specs=q_spec,
        scratch_shapes=[pltpu.VMEM((2, 1, 1), F32), pltpu.VMEM((tq, 1), F32), pltpu.VMEM((tq, 1), F32),
                        pltpu.VMEM((tq, 2 * DK_A), F32), pltpu.VMEM((tq, 2 * DK_A), F32)],
        compiler_params=_cparams(("arbitrary",) * 3),
        name="attn_prompt",
    )(lamv, qb, kbt, vb, sg)


def _attn_sample_kernel(pt_ref, lam_ref, q_ref, *rest, pages, page, past, t_dec, n_heads, lam_init):
    k_refs = rest[:pages]
    v_refs = rest[pages:2 * pages]
    kn_ref, vn_ref, sg_ref, o_ref, m_ref, acc_ref = rest[2 * pages:]
    j = pl.program_id(1)
    nrow = 2 * n_heads * t_dec
    ncol = page * n_heads

    @pl.when(j == 0)
    def _():
        m_ref[...] = jnp.full(m_ref.shape, NEG_BIG, F32)
        acc_ref[...] = jnp.zeros(acc_ref.shape, F32)

    q = q_ref[...]
    r = lax.broadcasted_iota(jnp.int32, (nrow, 1), 0)
    hrow = (r % (n_heads * t_dec)) // t_dec
    trow = r % t_dec
    slope2 = _alibi_slope(hrow, n_heads) * LOG2E
    col = lax.broadcasted_iota(jnp.int32, (1, ncol), 1)
    tok = col // n_heads
    same_head = (col % n_heads) == hrow
    base = jnp.where(same_head, slope2 * tok.astype(F32), NEG_BIG)

    for p in range(pages):
        off = slope2 * ((j * pages + p) * page - past).astype(F32)
        s = _dot_nt(q, k_refs[p][...].astype(BF16)) + (base + off)
        _softmax_step(s, _with_ones(v_refs[p][...].astype(BF16)), m_ref, acc_ref)

    @pl.when(j == pl.num_programs(1) - 1)
    def _():
        nnew = kn_ref.shape[0]
        ok = (tok[:, :nnew] <= trow) & (col[:, :nnew] < n_heads * t_dec)
        s = jnp.where(ok, _dot_nt(q, kn_ref[...]) + base[:, :nnew], NEG_BIG)
        _softmax_step(s, _with_ones(vn_ref[...]), m_ref, acc_ref)
        lam = _lambda_from(lam_ref, lam_init)
        acc = acc_ref[...]
        o = acc[:, :DK_A] / acc[:, DK_A:DK_A + 1]
        half = nrow // 2
        o = o[0:half] - lam * o[half:nrow]
        o = o * lax.rsqrt(jnp.mean(o * o, axis=-1, keepdims=True) + EPS)
        o_ref[...] = o * sg_ref[...] * (1.0 - lam_init)


def _attn_sample(pt_flat, lamv, q_rows, cache_kr, cache_vr, kn, vn, sg, n_pages, page, t_dec, past,
                 lam_init):
    bs, nrow, _ = q_rows.shape
    n_heads = nrow // (2 * t_dec)
    prow = page * n_heads
    pages = _largest_tile(n_pages, 16, 1)
    steps = n_pages // pages

    def page_spec(p):
        return pl.BlockSpec((prow, DK_A), lambda b, j, pt: (pt[b * n_pages + j * pages + p], 0))

    per_b = lambda a: pl.BlockSpec((None,) + a.shape[1:], lambda b, j, pt: (b, 0, 0))
    const = lambda a: pl.BlockSpec(a.shape, lambda b, j, pt: (0, 0))
    grid_spec = pltpu.PrefetchScalarGridSpec(
        num_scalar_prefetch=1,
        grid=(bs, steps),
        in_specs=[const(lamv), per_b(q_rows)] + [page_spec(p) for p in range(pages)]
                 + [page_spec(p) for p in range(pages)] + [per_b(kn), per_b(vn), const(sg)],
        out_specs=pl.BlockSpec((None, nrow // 2, DK_A), lambda b, j, pt: (b, 0, 0)),
        scratch_shapes=[pltpu.VMEM((nrow, 1), F32), pltpu.VMEM((nrow, 2 * DK_A), F32)],
    )
    return pl.pallas_call(
        functools.partial(_attn_sample_kernel, pages=pages, page=page, past=past, t_dec=t_dec,
                          n_heads=n_heads, lam_init=lam_init),
        out_shape=jax.ShapeDtypeStruct((bs, nrow // 2, DK_A), F32),
        grid_spec=grid_spec,
        compiler_params=_cparams(("arbitrary", "arbitrary")),
        name="attn_sample",
    )(pt_flat, lamv, q_rows, *([cache_kr] * pages), *([cache_vr] * pages), kn, vn, sg)


CONV_B_HALO = 32


def _convb_kernel(u_ref, buf_ref, w_ref, b_ref, lg_ref, lb_ref, cv_ref, nb_ref, xp_ref, xs_ref, *, tm):
    t = pl.program_id(1)
    pad = CONV_B_HALO - (CONV_B_WIDTH - 1)
    shifted_rows = tm + CONV_B_HALO - SUBLANES

    @pl.when(t == 0)
    def _():
        xp_ref[0:CONV_B_HALO] = buf_ref[...]

    @pl.when(t > 0)
    def _():
        xp_ref[0:CONV_B_HALO] = xp_ref[tm:tm + CONV_B_HALO]

    xp_ref[CONV_B_HALO:CONV_B_HALO + tm] = u_ref[...]
    for s in range(1, SUBLANES):
        xs_ref[s - 1, 0:shifted_rows] = xp_ref[s:s + shifted_rows]
    ch = min(64, tm)
    for c in range(tm // ch):
        acc = jnp.zeros((ch, u_ref.shape[1]), F32)
        for j in range(CONV_B_WIDTH):
            s = (pad + j) % SUBLANES
            s0 = c * ch + (pad + j) - s
            src = xp_ref if s == 0 else xs_ref.at[s - 1]
            acc = acc + w_ref[j:j + 1, :] * src[s0:s0 + ch, :]
        y = acc + b_ref[...]
        mu = jnp.mean(y, axis=-1, keepdims=True)
        yc = y - mu
        z = yc * lax.rsqrt(jnp.mean(yc * yc, axis=-1, keepdims=True) + EPS) * lg_ref[...] + lb_ref[...]
        cv_ref[c * ch:(c + 1) * ch] = (z * jax.nn.sigmoid(z)).astype(cv_ref.dtype)
    nb_ref[...] = xp_ref[tm:tm + CONV_B_HALO]


def _convb(u3, buf_pad, w_pad, b, lg, lb, out_dtype):
    nb, t, c = u3.shape
    tm = t if t < SUBLANES else _largest_tile(t, 256)
    nt = t // tm
    const = lambda a: pl.BlockSpec(a.shape, lambda bb, tt: (0, 0))
    return pl.pallas_call(
        functools.partial(_convb_kernel, tm=tm),
        out_shape=[jax.ShapeDtypeStruct((nb, t, c), out_dtype),
                   jax.ShapeDtypeStruct((nb, CONV_B_HALO, c), F32)],
        grid=(nb, nt),
        in_specs=[pl.BlockSpec((None, tm, c), lambda bb, tt: (bb, tt, 0)),
                  pl.BlockSpec((None, CONV_B_HALO, c), lambda bb, tt: (bb, 0, 0)),
                  const(w_pad), const(b), const(lg), const(lb)],
        out_specs=[pl.BlockSpec((None, tm, c), lambda bb, tt: (bb, tt, 0)),
                   pl.BlockSpec((None, CONV_B_HALO, c), lambda bb, tt: (bb, 0, 0))],
        scratch_shapes=[pltpu.VMEM((tm + CONV_B_HALO, c), F32),
                        pltpu.VMEM((SUBLANES - 1, -(-tm // SUBLANES) * SUBLANES + CONV_B_HALO - SUBLANES, c), F32)],
        compiler_params=_cparams(("arbitrary", "arbitrary")),
        name="conv_b",
    )(u3, buf_pad, w_pad, b, lg, lb)


def _outproj_even_kernel(o_ref, cv_ref, w_ref, x_ref, g1_ref, gf_ref, sc2_ref, sh2_ref,
                         x1_ref, h2_ref, *, aw):
    mix = (_dot(o_ref[...].astype(BF16), w_ref[0:aw, :])
           + _dot(cv_ref[...].astype(BF16), w_ref[aw:, :]))
    x1 = x_ref[...] + g1_ref[...] * mix
    x1_ref[...] = x1
    h2_ref[...] = _modnorm(x1, gf_ref[...], sc2_ref[...], sh2_ref[...])


def _outproj_even(o, cv, w_bf, x, tm, g1, gf, sc2, sh2, mod_map):
    rows, aw = o.shape
    d = x.shape[1]
    mr = g1.shape[1]
    const = lambda a: pl.BlockSpec(a.shape, lambda i: (0,) * a.ndim)
    mod_spec = pl.BlockSpec((None, mr, d), lambda i: (mod_map(i), 0, 0))
    xrow = pl.BlockSpec((tm, d), lambda i: (i, 0))
    return pl.pallas_call(
        functools.partial(_outproj_even_kernel, aw=aw),
        out_shape=[jax.ShapeDtypeStruct((rows, d), F32)] * 2,
        grid=(rows // tm,),
        in_specs=[pl.BlockSpec((tm, aw), lambda i: (i, 0)),
                  pl.BlockSpec((tm, cv.shape[1]), lambda i: (i, 0)),
                  const(w_bf), xrow, mod_spec, const(gf), mod_spec, mod_spec],
        out_specs=[xrow, xrow],
        compiler_params=_cparams(("arbitrary",)),
        name="outproj_even",
    )(o, cv, w_bf, x, g1, gf, sc2, sh2)


CONV_C_HALO = 8


def _odd_kernel(x_ref, gm_ref, sc1_ref, sh1_ref, win_ref, cw_ref, halo_ref, wout_ref, g1_ref,
                gf_ref, sc2_ref, sh2_ref, x1_ref, h2_ref, nb_ref, zp_ref,
                *, tm, cwid, seq_tiled, t_dec):
    x = x_ref[...]
    h = _modnorm(x, gm_ref[...], sc1_ref[...], sh1_ref[...]).astype(BF16)
    cg = _dot(h, win_ref[:, cwid:2 * cwid])
    hx = _dot(h, win_ref[:, 2 * cwid:3 * cwid])
    z = cg * hx
    halo = CONV_C_HALO
    if seq_tiled:
        t = pl.program_id(1)

        @pl.when(t == 0)
        def _():
            zp_ref[0:halo] = halo_ref[...]

        @pl.when(t > 0)
        def _():
            zp_ref[0:halo] = zp_ref[tm:tm + halo]
    else:
        zp_ref[0:halo] = jnp.zeros((halo, cwid), F32)
    zp_ref[halo:halo + tm] = z
    prev1 = zp_ref[halo - 1:halo - 1 + tm]
    prev2 = zp_ref[halo - 2:halo - 2 + tm]
    if seq_tiled:
        nb_ref[...] = zp_ref[tm:tm + halo]
    else:
        tpos = lax.broadcasted_iota(jnp.int32, (tm, 1), 0) % t_dec
        prev1 = jnp.where(tpos >= 1, prev1, halo_ref[0])
        prev2 = jnp.where(tpos >= 2, prev2, halo_ref[1])
        nb_ref[...] = z
    y = cw_ref[0:1, :] * prev2 + cw_ref[1:2, :] * prev1 + cw_ref[2:3, :] * z
    bg = _dot(h, win_ref[:, 0:cwid])
    mix = _dot((bg * y).astype(BF16), wout_ref[...])
    x1 = x + g1_ref[...] * mix
    x1_ref[...] = x1
    h2_ref[...] = _modnorm(x1, gf_ref[...], sc2_ref[...], sh2_ref[...])


def _odd_mixer(x, nb, t, tm, gm, sc1, sh1, mod_map, win_bf, cw, halo, wout_bf, g1,
               gf, sc2, sh2, seq_tiled, t_dec):
    rows, d = x.shape
    cwid = wout_bf.shape[0]
    mr = g1.shape[1]
    if seq_tiled:
        nt = t // tm
        grid = (nb, nt)
        xmap = lambda b, i: (b * nt + i, 0)
        halo_spec = pl.BlockSpec((None, CONV_C_HALO, cwid), lambda b, i: (b, 0, 0))
        nb_shape = jax.ShapeDtypeStruct((nb, CONV_C_HALO, cwid), F32)
        nb_spec = pl.BlockSpec((None, CONV_C_HALO, cwid), lambda b, i: (b, 0, 0))
        mmap = lambda b, i: (mod_map(b, i), 0, 0)
    else:
        grid = (1, 1)
        xmap = lambda b, i: (0, 0)
        halo_spec = pl.BlockSpec(halo.shape, lambda b, i: (0, 0, 0))
        nb_shape = jax.ShapeDtypeStruct((tm, cwid), F32)
        nb_spec = pl.BlockSpec((tm, cwid), lambda b, i: (0, 0))
        mmap = lambda b, i: (0, 0, 0)
    const = lambda a: pl.BlockSpec(a.shape, lambda b, i: (0,) * a.ndim)
    mod_spec = pl.BlockSpec((None, mr, d), mmap)
    xrow = pl.BlockSpec((tm, d), xmap)
    return pl.pallas_call(
        functools.partial(_odd_kernel, tm=tm, cwid=cwid, seq_tiled=seq_tiled, t_dec=t_dec),
        out_shape=[jax.ShapeDtypeStruct((rows, d), F32)] * 2 + [nb_shape],
        grid=grid,
        in_specs=[xrow, const(gm), mod_spec, mod_spec, const(win_bf), const(cw), halo_spec,
                  const(wout_bf), mod_spec, const(gf), mod_spec, mod_spec],
        out_specs=[xrow, xrow, nb_spec],
        scratch_shapes=[pltpu.VMEM((tm + CONV_C_HALO, cwid), F32)],
        compiler_params=_cparams(("arbitrary", "arbitrary")),
        name="odd_mixer",
    )(x, gm, sc1, sh1, win_bf, cw, halo, wout_bf, g1, gf, sc2, sh2)


def _pack_bf16_pairs(a):
    half = a.shape[1] // 2
    lo = lax.bitcast_convert_type(a[:, :half].astype(F32), jnp.uint32) >> 16
    hi = lax.bitcast_convert_type(a[:, half:].astype(F32), jnp.uint32) & jnp.uint32(0xFFFF0000)
    return lo | hi


def _unpack_bf16_pairs(w):
    lo = lax.bitcast_convert_type(w << 16, F32).astype(BF16)
    hi = lax.bitcast_convert_type(w & jnp.uint32(0xFFFF0000), F32).astype(BF16)
    return lo, hi


def _route_kernel(h_ref, w01_ref, w0_ref, b_ref, tri_ref, cnt0_ref, meta_ref, cnt_ref, hp_ref,
                  carry_ref):
    i = pl.program_id(0)

    @pl.when(i == 0)
    def _():
        carry_ref[...] = cnt0_ref[...]

    h = h_ref[...]
    a0 = h.astype(BF16)
    a1 = (h - a0.astype(F32)).astype(BF16)
    t01 = _dot(a0, w01_ref[...])
    logits = t01[:, :LANES] + t01[:, LANES:] + _dot(a1, w0_ref[...]) + b_ref[...]
    hp_ref[...] = _pack_bf16_pairs(a0)
    tm = logits.shape[0]
    lane = lax.broadcasted_iota(jnp.int32, logits.shape, 1)
    lane_f = lane.astype(F32)
    neg = jnp.float32(-jnp.inf)

    def first_max(vals):
        vmax = jnp.max(vals, axis=-1, keepdims=True)
        idx = jnp.min(jnp.where(vals == vmax, lane_f, float(LANES)), axis=-1, keepdims=True)
        return vmax, idx.astype(jnp.int32)

    gmask = lane < N_GROUPS
    gmax, g_idx = first_max(jnp.where(gmask, logits, neg))
    g_w = 1.0 / jnp.sum(jnp.where(gmask, jnp.exp(logits - gmax), 0.0), axis=-1, keepdims=True)
    lo = N_GROUPS + EXPERTS_PER_GROUP * g_idx
    el = jnp.where((lane >= lo) & (lane < lo + EXPERTS_PER_GROUP), logits, neg)
    v1, i1 = first_max(el)
    v2, i2 = first_max(jnp.where(lane == i1, neg, el))
    tt = jnp.exp(v2 - v1)
    wgt1 = g_w / (1.0 + tt)
    wgt2 = g_w * tt / (1.0 + tt)
    e1 = i1 - N_GROUPS
    e2 = i2 - N_GROUPS
    hit1 = lane == e1
    hit2 = lane == e2
    onehot = (hit1 | hit2).astype(F32)
    cnt = _dot(tri_ref[...], onehot.astype(BF16)) + carry_ref[...]
    r1 = jnp.sum(jnp.where(hit1, cnt, 0.0), axis=-1, keepdims=True)
    r2 = jnp.sum(jnp.where(hit2, cnt, 0.0), axis=-1, keepdims=True)
    carry_ref[...] = carry_ref[...] + jnp.sum(onehot, axis=0, keepdims=True)
    cnt_ref[...] = carry_ref[...]
    cols = (e1.astype(F32), e2.astype(F32), r1, r2, wgt1, wgt2)
    meta = jnp.zeros((tm, LANES), F32)
    for c, val in enumerate(cols):
        meta = jnp.where(lane == c, val, meta)
    meta_ref[...] = meta


def _route(h2, w01, w0, b_pad, cnt0, tm):
    n, d = h2.shape
    tri = (jnp.arange(tm)[:, None] > jnp.arange(tm)[None, :]).astype(BF16)
    const = lambda a: pl.BlockSpec(a.shape, lambda i: (0, 0))
    return pl.pallas_call(
        _route_kernel,
        out_shape=[jax.ShapeDtypeStruct((n, LANES), F32), jax.ShapeDtypeStruct((1, LANES), F32),
                   jax.ShapeDtypeStruct((n, d // 2), jnp.uint32)],
        grid=(n // tm,),
        in_specs=[pl.BlockSpec((tm, d), lambda i: (i, 0)), const(w01), const(w0), const(b_pad),
                  const(tri), const(cnt0)],
        out_specs=[pl.BlockSpec((tm, LANES), lambda i: (i, 0)), const(cnt0),
                   pl.BlockSpec((tm, d // 2), lambda i: (i, 0))],
        scratch_shapes=[pltpu.VMEM((1, LANES), F32)],
        compiler_params=_cparams(("arbitrary",)),
        name="moe_route",
    )(h2, w01, w0, b_pad, tri, cnt0)


def _dispatch_kernel(dest_ref, h_ref, xp_in_ref, xp_ref, sem, *, tm):
    del xp_in_ref

    def row_copy(r, slot):
        return pltpu.make_async_copy(h_ref.at[pl.ds(r, 1)], xp_ref.at[pl.ds(slot, 1)], sem)

    def issue(r, c):
        for k in range(TOP_K):
            row_copy(r, dest_ref[0, TOP_K * r + k]).start()
        return c

    lax.fori_loop(0, tm, issue, 0, unroll=DMA_UNROLL)

    def drain(r, c):
        for k in range(TOP_K):
            row_copy(r, dest_ref[0, TOP_K * r + k]).wait()
        return c

    lax.fori_loop(0, tm, drain, 0, unroll=DMA_UNROLL)


def _dispatch(dest3, h2_all, xp_zero, tm):
    n, d = h2_all.shape
    return pl.pallas_call(
        functools.partial(_dispatch_kernel, tm=tm),
        out_shape=jax.ShapeDtypeStruct(xp_zero.shape, xp_zero.dtype),
        grid=(n // tm,),
        in_specs=[pl.BlockSpec((None, 1, TOP_K * tm), lambda i: (i, 0, 0), memory_space=pltpu.SMEM),
                  pl.BlockSpec((tm, d), lambda i: (i, 0)),
                  pl.BlockSpec(memory_space=pl.ANY)],
        out_specs=pl.BlockSpec(memory_space=pl.ANY),
        input_output_aliases={2: 0},
        scratch_shapes=[pltpu.SemaphoreType.DMA],
        compiler_params=_cparams(("arbitrary",)),
        name="moe_dispatch",
    )(dest3, h2_all, xp_zero)


def _expert_kernel(be_ref, nu_ref, x_ref, wgu_ref, wd_ref, y_ref, wgu_bf, wd_bf, *, f):
    b = pl.program_id(0)
    prev = be_ref[jnp.maximum(b - 1, 0)]

    @pl.when((b == 0) | (be_ref[b] != prev))
    def _():
        wgu_bf[...] = wgu_ref[...].astype(BF16)
        wd_bf[...] = wd_ref[...].astype(BF16)

    @pl.when(b < nu_ref[0])
    def _():
        x_lo, x_hi = _unpack_bf16_pairs(x_ref[...])
        half = x_lo.shape[1]
        gate = _dot(x_lo, wgu_bf[0:half, 0:f]) + _dot(x_hi, wgu_bf[half:, 0:f])
        up = _dot(x_lo, wgu_bf[0:half, f:2 * f]) + _dot(x_hi, wgu_bf[half:, f:2 * f])
        act = (gate * jax.nn.sigmoid(gate) * up).astype(BF16)
        y_ref[...] = _dot(act, wd_bf[...])

    @pl.when(b >= nu_ref[0])
    def _():
        y_ref[...] = jnp.zeros(y_ref.shape, F32)


def _experts(block_e, n_used, xp, w_gate_up, w_down, layer):
    n_slots, dh = xp.shape
    d = 2 * dh
    f = w_down.shape[2]
    grid_spec = pltpu.PrefetchScalarGridSpec(
        num_scalar_prefetch=2,
        grid=(n_slots // MOE_ROWS,),
        in_specs=[pl.BlockSpec((MOE_ROWS, dh), lambda b, be, nu: (b, 0)),
                  pl.BlockSpec((None, None, d, 2 * f), lambda b, be, nu: (layer, be[b], 0, 0)),
                  pl.BlockSpec((None, None, f, d), lambda b, be, nu: (layer, be[b], 0, 0))],
        out_specs=pl.BlockSpec((MOE_ROWS, d), lambda b, be, nu: (b, 0)),
        scratch_shapes=[pltpu.VMEM((d, 2 * f), BF16), pltpu.VMEM((f, d), BF16)],
    )
    return pl.pallas_call(
        functools.partial(_expert_kernel, f=f),
        out_shape=jax.ShapeDtypeStruct((n_slots, d), F32),
        grid_spec=grid_spec,
        compiler_params=_cparams(("arbitrary",)),
        name="moe_experts",
    )(block_e, n_used, xp, w_gate_up, w_down)


def _combine_kernel(dest_ref, x_ref, meta_ref, g2_ref, yp_ref, o_ref, ybuf, sem, *, tm):

    def row_copy(r, k, slot):
        return pltpu.make_async_copy(yp_ref.at[pl.ds(slot, 1)], ybuf.at[k, pl.ds(r, 1)], sem)

    def issue(r, c):
        for k in range(TOP_K):
            row_copy(r, k, dest_ref[0, TOP_K * r + k]).start()
        return c

    lax.fori_loop(0, tm, issue, 0, unroll=DMA_UNROLL)

    def drain(r, c):
        for k in range(TOP_K):
            row_copy(r, k, dest_ref[0, TOP_K * r + k]).wait()
        return c

    lax.fori_loop(0, tm, drain, 0, unroll=DMA_UNROLL)
    meta = meta_ref[...]
    y = meta[:, 4:5] * ybuf[0] + meta[:, 5:6] * ybuf[1]
    o_ref[...] = x_ref[...] + g2_ref[...] * y


def _combine(dest3, x1, meta, g2, mod_map, yp, tm):
    rows, d = x1.shape
    mr = g2.shape[1]
    xrow = pl.BlockSpec((tm, d), lambda i: (i, 0))
    return pl.pallas_call(
        functools.partial(_combine_kernel, tm=tm),
        out_shape=jax.ShapeDtypeStruct((rows, d), F32),
        grid=(rows // tm,),
        in_specs=[pl.BlockSpec((None, 1, TOP_K * tm), lambda i: (i, 0, 0), memory_space=pltpu.SMEM),
                  xrow, pl.BlockSpec((tm, LANES), lambda i: (i, 0)),
                  pl.BlockSpec((None, mr, d), lambda i: (mod_map(i), 0, 0)),
                  pl.BlockSpec(memory_space=pl.ANY)],
        out_specs=xrow,
        scratch_shapes=[pltpu.VMEM((TOP_K, tm, d), F32), pltpu.SemaphoreType.DMA],
        compiler_params=_cparams(("arbitrary",)),
        name="moe_combine",
    )(dest3, x1, meta, g2, yp)


def _moe(h2_p, h2_s, x1_p, x1_s, g2_p, g2_s, w01, w0, b_pad, w_gate_up, w_down, layer, tm_p, seq):
    n_p, d = h2_p.shape
    n_s = h2_s.shape[0]
    n = n_p + n_s
    meta_p, cnt_p, hp_p = _route(h2_p, w01, w0, b_pad, jnp.zeros((1, LANES), F32), tm_p)
    meta_s, counts, hp_s = _route(h2_s, w01, w0, b_pad, cnt_p, n_s)
    sizes = counts[0, :N_EXPERTS].astype(jnp.int32)
    padded = ((sizes + MOE_ROWS - 1) // MOE_ROWS) * MOE_ROWS
    pad_end = jnp.cumsum(padded)
    pad_start = pad_end - padded

    def slots(meta):
        e_tok = meta[:, 0:TOP_K].astype(jnp.int32)
        rank = meta[:, TOP_K:2 * TOP_K].astype(jnp.int32)
        start = jnp.sum(jnp.where(e_tok[:, :, None] == jnp.arange(N_EXPERTS)[None, None, :],
                                  pad_start[None, None, :], 0), axis=-1)
        return (start + rank).astype(jnp.int32)

    dest_p = slots(meta_p).reshape(n_p // tm_p, 1, TOP_K * tm_p)
    dest_s = slots(meta_s).reshape(1, 1, TOP_K * n_s)
    n_blocks = (n * TOP_K + MOE_ROWS - 1) // MOE_ROWS + N_EXPERTS
    blk0 = jnp.arange(n_blocks, dtype=jnp.int32) * MOE_ROWS
    block_e = jnp.minimum(jnp.sum(pad_end[None, :] <= blk0[:, None], axis=-1), N_EXPERTS - 1).astype(jnp.int32)
    n_used = (pad_end[-1:] // MOE_ROWS).astype(jnp.int32)

    xp = _dispatch(dest_p, hp_p, jnp.zeros((n_blocks * MOE_ROWS, d // 2), jnp.uint32), tm_p)
    xp = _dispatch(dest_s, hp_s, xp, n_s)
    yp = _experts(block_e, n_used, xp, w_gate_up, w_down, layer)
    per_seq = seq // tm_p
    out_p = _combine(dest_p, x1_p, meta_p, g2_p, lambda i: i // per_seq, yp, tm_p)
    out_s = _combine(dest_s, x1_s, meta_s, g2_s, lambda i: 0, yp, n_s)
    return out_p, out_s


def kernel(x_prompt, x_sample, cache_k, cache_v, state_conv_b, state_conv_c, page_table, c_prompt, c_sample, norm_mix_g, norm_ffn_g, w_ada, b_ada, w_in_a, q_norm_g, k_norm_g, lam_q1, lam_k1, lam_q2, lam_k2, subln_g, convb_w, convb_b, convb_ln_g, convb_ln_b, w_out_a, w_in_c, convc_w, w_out_c, w_group, b_group, w_router, b_router, w_gate_up, w_down):
    bp, seq, d = x_prompt.shape
    bs, t_dec, _ = x_sample.shape
    depth = w_ada.shape[0]
    n_p, n_s = bp * seq, bs * t_dec
    n = n_p + n_s
    n_pool, page = cache_k.shape[1], cache_k.shape[2]
    n_heads = cache_k.shape[3]
    aw = n_heads * DK_A
    bw = state_conv_b.shape[-1]
    cwid = state_conv_c.shape[-1]
    n_pages = page_table.shape[1]
    past = n_pages * page
    tm_p = _largest_tile(seq, 512, LANES)
    assert n_s % SUBLANES == 0

    x_p = x_prompt.reshape(n_p, d)
    x_s = x_sample.reshape(n_s, d)

    c_all = jnp.concatenate([c_prompt, c_sample], axis=0)
    rows = c_all.shape[0]
    rows_pad = -(-rows // SUBLANES) * SUBLANES
    mods = _ada_all(jnp.pad(c_all, ((0, rows_pad - rows), (0, 0))), w_ada, b_ada)
    mods = mods[:, :rows].reshape(depth, rows, 6, d)

    def mod_rows(l, j):
        m_p = mods[l, :bp, j][:, None, :]
        m_s = jnp.repeat(mods[l, bp:, j], t_dec, axis=0)[None]
        return m_p, m_s

    bd = (jnp.arange(aw)[:, None] // HD_A == jnp.arange(aw)[None, :] // HD_A).astype(BF16)
    cache_kr = cache_k.reshape(-1, DK_A)
    cache_vr = cache_v.reshape(-1, DK_A)
    per_seq = seq // tm_p
    pmap1 = lambda i: i // per_seq
    smap1 = lambda i: 0
    row2 = lambda a: a.reshape(1, -1)

    k_p, v_p, k_s, v_s, cb_p, cb_s, cc_p, cc_s = [], [], [], [], [], [], [], []
    for l in range(depth):
        i = l // 2
        sh1_p, sh1_s = mod_rows(l, 0)
        sc1_p, sc1_s = mod_rows(l, 1)
        g1_p, g1_s = mod_rows(l, 2)
        sh2_p, sh2_s = mod_rows(l, 3)
        sc2_p, sc2_s = mod_rows(l, 4)
        g2_p, g2_s = mod_rows(l, 5)
        gm = row2(norm_mix_g[l])
        gf = row2(norm_ffn_g[l])
        if l % 2 == 0:
            lam_init = 0.8 - 0.6 * math.exp(-0.3 * l)
            w_in = w_in_a[i].astype(BF16)
            w_out = w_out_a[i].astype(BF16)
            qg = jnp.tile(q_norm_g[i].reshape(1, DK_A), (1, n_heads))
            kg = jnp.tile(k_norm_g[i].reshape(1, DK_A), (1, n_heads))
            lamv = jnp.stack([lam_q1[i], lam_k1[i], lam_q2[i], lam_k2[i]])
            sg = row2(subln_g[i])
            cw = jnp.pad(convb_w[i].reshape(CONV_B_WIDTH, bw), ((0, 1), (0, 0)))
            cbb, clg, clb = row2(convb_b[i]), row2(convb_ln_g[i]), row2(convb_ln_b[i])
            halo_pad = ((0, 0), (CONV_B_HALO - (CONV_B_WIDTH - 1), 0), (0, 0))

            qb, kf, kbt, vf, vb, u = _inproj_even(x_p, tm_p, gm, sc1_p, sh1_p, pmap1,
                                                  w_in, bd, qg, kg, aw, bw, True)
            o_p = _attn_prompt(lamv, qb, kbt, vb, sg, bp, seq, lam_init)
            cv_p, nb_p = _convb(u.reshape(bp, seq, bw), jnp.zeros((bp, CONV_B_HALO, bw), F32),
                                cw, cbb, clg, clb, BF16)
            x1_p, h2_p = _outproj_even(o_p, cv_p.reshape(n_p, bw), w_out, x_p, tm_p,
                                       g1_p, gf, sc2_p, sh2_p, pmap1)
            k_p.append(kf.reshape(bp, seq, n_heads, DK_A))
            v_p.append(vf.reshape(bp, seq, n_heads, DK_A))
            cb_p.append(nb_p[:, CONV_B_HALO - (CONV_B_WIDTH - 1):])

            qb, kf, kb, vf, vb, u = _inproj_even(x_s, n_s, gm, sc1_s, sh1_s, smap1,
                                                 w_in, bd, qg, kg, aw, bw, False)
            q4 = qb.reshape(bs, t_dec, n_heads, 2, HD_A).transpose(0, 2, 1, 3, 4)[:, None]
            eye_m = jnp.eye(2, dtype=BF16)[None, :, None, None, :, None]
            q_rows = (q4 * eye_m).reshape(bs, 2 * n_heads * t_dec, DK_A)
            new_pad = ((0, 0), (0, LANES - t_dec * n_heads), (0, 0))
            kn = jnp.pad(kb.reshape(bs, t_dec * n_heads, DK_A), new_pad)
            vn = jnp.pad(vb.reshape(bs, t_dec * n_heads, DK_A), new_pad)
            pt_flat = (page_table.astype(jnp.int32) + i * n_pool).reshape(-1)
            o_s = _attn_sample(pt_flat, lamv, q_rows, cache_kr, cache_vr, kn, vn, sg, n_pages, page,
                               t_dec, past, lam_init)
            o_s = o_s.reshape(bs, n_heads, t_dec, DK_A).transpose(0, 2, 1, 3)
            cv_s, nb_s = _convb(u.reshape(bs, t_dec, bw), jnp.pad(state_conv_b[i], halo_pad),
                                cw, cbb, clg, clb, F32)
            x1_s, h2_s = _outproj_even(o_s.reshape(n_s, aw), cv_s.reshape(n_s, bw), w_out,
                                       x_s, n_s, g1_s, gf, sc2_s, sh2_s, smap1)
            k_s.append(kf.reshape(bs, t_dec, n_heads, DK_A))
            v_s.append(vf.reshape(bs, t_dec, n_heads, DK_A))
            cb_s.append(nb_s[:, CONV_B_HALO - (CONV_B_WIDTH - 1):])
        else:
            w_in = w_in_c[i].astype(BF16)
            w_out = w_out_c[i].astype(BF16)
            cw = jnp.pad(convc_w[i].reshape(CONV_C_WIDTH, cwid), ((0, SUBLANES - CONV_C_WIDTH), (0, 0)))
            halo_p = jnp.zeros((bp, CONV_C_HALO, cwid), F32)
            x1_p, h2_p, nb_p = _odd_mixer(
                x_p, bp, seq, tm_p, gm, sc1_p, sh1_p, lambda b, t: b, w_in, cw, halo_p, w_out,
                g1_p, gf, sc2_p, sh2_p, True, t_dec)
            cc_p.append(nb_p[:, CONV_C_HALO - (CONV_C_WIDTH - 1):])
            st = state_conv_c[i]
            zrow = jnp.zeros((bs, 1, cwid), F32)
            halo1 = jnp.concatenate([st[:, 1:2]] + [zrow] * (t_dec - 1), axis=1).reshape(n_s, cwid)
            halo2 = jnp.concatenate([st[:, 0:1], st[:, 1:2]] + [zrow] * (t_dec - 2), axis=1).reshape(n_s, cwid)
            x1_s, h2_s, z_s = _odd_mixer(
                x_s, 1, n_s, n_s, gm, sc1_s, sh1_s, None, w_in, cw, jnp.stack([halo1, halo2]),
                w_out, g1_s, gf, sc2_s, sh2_s, False, t_dec)
            cc_s.append(z_s.reshape(bs, t_dec, cwid)[:, t_dec - (CONV_C_WIDTH - 1):])

        w_r = jnp.pad(jnp.concatenate([w_group[l], w_router[l]], axis=1),
                      ((0, 0), (0, LANES - N_GROUPS - N_EXPERTS)))
        w0 = w_r.astype(BF16)
        w1 = (w_r - w0.astype(F32)).astype(BF16)
        w01 = jnp.concatenate([w0, w1], axis=1)
        b_r = jnp.pad(jnp.concatenate([b_group[l], b_router[l]]), (0, LANES - N_GROUPS - N_EXPERTS))
        x_p, x_s = _moe(h2_p, h2_s, x1_p, x1_s, g2_p, g2_s, w01, w0, row2(b_r),
                        w_gate_up, w_down, l, tm_p, seq)

    return (x_p.reshape(bp, seq, d), x_s.reshape(bs, t_dec, d),
            jnp.stack(k_p), jnp.stack(v_p), jnp.stack(k_s), jnp.stack(v_s),
            jnp.stack(cb_p), jnp.stack(cb_s), jnp.stack(cc_p), jnp.stack(cc_s))
```

```python
import functools
import math

import jax
import jax.numpy as jnp
from jax import lax
from jax.experimental import pallas as pl
from jax.experimental.pallas import tpu as pltpu

F32 = jnp.float32
BF16 = jnp.bfloat16

HD_A = 64
DK_A = 2 * HD_A
N_GROUPS = 4
EXPERTS_PER_GROUP = 8
N_EXPERTS = N_GROUPS * EXPERTS_PER_GROUP
TOP_K = 2
CONV_B_WIDTH = 31
CONV_C_WIDTH = 3
ALIBI_MAX = 8.0
EPS = 1e-6
NEG_BIG = -1e30
LOG2E = math.log2(math.e)

LANES = 128
SUBLANES = 8
MOE_ROWS = 256
STAB_LIMIT = 60.0
BOUND_SLACK = 1.01
ATTN_TILE = 512
DMA_UNROLL = 8
VMEM_LIMIT = 56 * 1024 * 1024


def _cparams(sem, vmem=VMEM_LIMIT):
    return pltpu.CompilerParams(dimension_semantics=sem, vmem_limit_bytes=vmem)


def _largest_tile(n, cap, mult=SUBLANES):
    best = None
    for t in range(mult, min(n, cap) + 1, mult):
        if n % t == 0:
            best = t
    assert best is not None, (n, cap, mult)
    return best


def _modnorm(x, g, scale, shift):
    ms = jnp.mean(x * x, axis=-1, keepdims=True)
    return (x * lax.rsqrt(ms + EPS)) * g * (1.0 + scale) + shift


def _dot(a, b):
    return jnp.dot(a, b, preferred_element_type=F32)


def _dot_nt(a, b):
    return lax.dot_general(a, b, (((1,), (1,)), ((), ())), preferred_element_type=F32)


def _split3(a):
    a0 = a.astype(BF16)
    r1 = a - a0.astype(F32)
    a1 = r1.astype(BF16)
    a2 = (r1 - a1.astype(F32)).astype(BF16)
    return a0, a1, a2


def _ada_kernel(c_ref, w_ref, b_ref, o_ref):
    c = c_ref[...]
    a = (c * jax.nn.sigmoid(c)).astype(BF16)
    o_ref[...] = _dot(a, w_ref[...].astype(BF16)) + b_ref[...]


def _ada_all(c_pad, w_ada, b_ada):
    depth, d, n6 = w_ada.shape
    rows = c_pad.shape[0]
    tn = 1536
    return pl.pallas_call(
        _ada_kernel,
        out_shape=jax.ShapeDtypeStruct((depth, rows, n6), F32),
        grid=(depth, n6 // tn),
        in_specs=[pl.BlockSpec((rows, d), lambda l, j: (0, 0)),
                  pl.BlockSpec((None, d, tn), lambda l, j: (l, 0, j)),
                  pl.BlockSpec((None, 1, tn), lambda l, j: (l, 0, j))],
        out_specs=pl.BlockSpec((None, rows, tn), lambda l, j: (l, 0, j)),
        compiler_params=_cparams(("arbitrary", "arbitrary")),
        name="ada_mod",
    )(c_pad, w_ada, b_ada.reshape(depth, 1, n6))


def _inproj_even_kernel(x_ref, g_ref, sc_ref, sh_ref, w_ref, bd_ref, qg_ref, kg_ref,
                        qb_ref, kf_ref, kb_ref, vf_ref, vb_ref, u_ref, *, aw, bw, k_transposed):
    h = _modnorm(x_ref[...], g_ref[...], sc_ref[...], sh_ref[...]).astype(BF16)
    bd = bd_ref[...]

    def group_norm(t, gain):
        tt = t * t
        hi = tt.astype(BF16)
        lo = (tt - hi.astype(F32)).astype(BF16)
        ss = _dot(hi, bd) + _dot(lo, bd)
        return t * lax.rsqrt(ss * (1.0 / HD_A) + EPS) * gain

    n_heads = aw // DK_A
    tm = x_ref.shape[0]

    def store_head_rows(ref, t):
        for hh in range(n_heads):
            ref[pl.ds(hh, tm, stride=n_heads), :] = t[:, hh * DK_A:(hh + 1) * DK_A]

    q = group_norm(_dot(h, w_ref[:, 0:aw]), qg_ref[...])
    qb_ref[...] = (q * (HD_A ** -0.5 * LOG2E)).astype(BF16)
    k = group_norm(_dot(h, w_ref[:, aw:2 * aw]), kg_ref[...])
    store_head_rows(kf_ref, k)
    kb_ref[...] = (k.T if k_transposed else k).astype(BF16)
    v = _dot(h, w_ref[:, 2 * aw:3 * aw])
    store_head_rows(vf_ref, v)
    vb_ref[...] = v.astype(BF16)
    ga = _dot(h, w_ref[:, 3 * aw:3 * aw + bw])
    gb = _dot(h, w_ref[:, 3 * aw + bw:3 * aw + 2 * bw])
    u_ref[...] = ga * jax.nn.sigmoid(gb)


def _inproj_even(x, tm, g, sc, sh, mod_map, w_bf, bd, qg, kg, aw, bw, k_transposed):
    rows, d = x.shape
    nt = rows // tm
    mr = sc.shape[1]
    n_heads = aw // DK_A
    row_spec = lambda width: pl.BlockSpec((tm, width), lambda i: (i, 0))
    head_rows = pl.BlockSpec((tm * n_heads, DK_A), lambda i: (i, 0))
    const = lambda a: pl.BlockSpec(a.shape, lambda i: (0,) * a.ndim)
    mod_spec = pl.BlockSpec((None, mr, d), lambda i: (mod_map(i), 0, 0))
    kb_shape = (aw, rows) if k_transposed else (rows, aw)
    kb_spec = pl.BlockSpec((aw, tm), lambda i: (0, i)) if k_transposed else row_spec(aw)
    outs = [jax.ShapeDtypeStruct((rows, aw), BF16), jax.ShapeDtypeStruct((rows * n_heads, DK_A), F32),
            jax.ShapeDtypeStruct(kb_shape, BF16), jax.ShapeDtypeStruct((rows * n_heads, DK_A), F32),
            jax.ShapeDtypeStruct((rows, aw), BF16), jax.ShapeDtypeStruct((rows, bw), F32)]
    return pl.pallas_call(
        functools.partial(_inproj_even_kernel, aw=aw, bw=bw, k_transposed=k_transposed),
        out_shape=outs,
        grid=(nt,),
        in_specs=[pl.BlockSpec((tm, d), lambda i: (i, 0)), const(g), mod_spec, mod_spec,
                  const(w_bf), const(bd), const(qg), const(kg)],
        out_specs=[row_spec(aw), head_rows, kb_spec, head_rows, row_spec(aw), row_spec(bw)],
        compiler_params=_cparams(("arbitrary",)),
        name="inproj_even",
    )(x, g, sc, sh, w_bf, bd, qg, kg)


def _lambda_from(lam_ref, lam_init):
    lv = lam_ref[...]
    a = jnp.sum(lv[0:1] * lv[1:2], axis=-1, keepdims=True)
    b = jnp.sum(lv[2:3] * lv[3:4], axis=-1, keepdims=True)
    return jnp.exp(a) - jnp.exp(b) + lam_init


def _alibi_slope(head, n_heads):
    slope = jnp.float32(0.0)
    for hh in range(n_heads):
        slope = jnp.where(head == hh, jnp.float32(2.0 ** (-ALIBI_MAX * (hh + 1) / n_heads)), slope)
    return slope


def _softmax_step(s, v_ones, m_ref, acc_ref):
    m_old = m_ref[...]
    m_new = jnp.maximum(m_old, jnp.max(s, axis=-1, keepdims=True))
    p = jnp.exp2(s - m_new)
    acc_ref[...] = jnp.exp2(m_old - m_new) * acc_ref[...] + _dot(p.astype(BF16), v_ones)
    m_ref[...] = m_new


def _with_ones(v):
    return jnp.concatenate([v, jnp.ones(v.shape, v.dtype)], axis=-1)


def _attn_prompt_kernel(lam_ref, q_ref, kt_ref, v_ref, sg_ref, o_ref, kmax_ref, m1_ref, m2_ref,
                        acc1_ref, acc2_ref, *, tq, lam_init, n_heads):
    h = pl.program_id(1)
    qi = pl.program_id(2)

    @pl.when(qi == 0)
    def _():
        kk = kt_ref[...].astype(F32)
        kk = kk * kk
        for m in range(2):
            n2 = jnp.sum(kk[m * HD_A:(m + 1) * HD_A], axis=0, keepdims=True)
            kmax_ref[m] = jnp.sqrt(jnp.max(n2, axis=-1, keepdims=True))

    for acc_ref in (acc1_ref, acc2_ref):
        acc_ref[...] = jnp.zeros(acc_ref.shape, F32)
    slope2 = _alibi_slope(h, n_heads) * LOG2E
    q = q_ref[...]
    lane = lax.broadcasted_iota(jnp.int32, q.shape, 1)
    zero = jnp.zeros_like(q)
    q1 = jnp.where(lane < HD_A, q, zero)
    q2 = jnp.where(lane >= HD_A, q, zero)
    col = lax.broadcasted_iota(jnp.int32, (1, tq), 1)
    bound = []
    for m, qm in enumerate((q1, q2)):
        qf = qm.astype(F32)
        qn = jnp.sqrt(jnp.sum(qf * qf, axis=-1, keepdims=True))
        bound.append(qn * kmax_ref[m] * BOUND_SLACK)
    bounded = jnp.max(jnp.maximum(bound[0], bound[1])) < STAB_LIMIT

    def tiles(kj):
        r0 = pl.multiple_of(kj * tq, tq)
        colb = slope2 * ((kj - qi) * tq + col).astype(F32)
        return kt_ref[:, pl.ds(r0, tq)], _with_ones(v_ref[pl.ds(r0, tq), :]), colb

    def causal_keep():
        return (lax.broadcasted_iota(jnp.int32, (tq, tq), 0)
                >= lax.broadcasted_iota(jnp.int32, (tq, tq), 1))

    def run(step):
        def body(kj, carry):
            step(kj, False)
            return carry

        lax.fori_loop(0, qi, body, 0)
        step(qi, True)

    @pl.when(bounded)
    def _():
        row = lax.broadcasted_iota(jnp.int32, (tq, 1), 0).astype(F32)
        lane2 = lax.broadcasted_iota(jnp.int32, (tq, LANES), 1)
        sub = lax.broadcasted_iota(jnp.int32, (2 * SUBLANES, tq), 0)
        q_aug = []
        for qm, bm in ((q1, bound[0]), (q2, bound[1])):
            rt = -(bm + slope2 * row)
            rt_hi = rt.astype(BF16).astype(F32)
            rt_lo = (rt - rt_hi).astype(BF16).astype(F32)
            ext = jnp.where(lane2 == 0, rt_hi,
                            jnp.where(lane2 == 1, rt_lo, jnp.where(lane2 < 5, 1.0, 0.0)))
            q_aug.append(jnp.concatenate([qm, ext.astype(BF16)], axis=-1))
        k_pad = jnp.zeros((LANES - 2 * SUBLANES, tq), BF16)

        def step(kj, masked):
            kt, v_ones, colb = tiles(kj)
            c0, c1, c2 = (c.astype(F32) for c in _split3(colb))
            ext_k = jnp.where(sub < 2, 1.0,
                              jnp.where(sub == 2, c0,
                                        jnp.where(sub == 3, c1, jnp.where(sub == 4, c2, 0.0))))
            kt_aug = jnp.concatenate([kt, ext_k.astype(BF16), k_pad], axis=0)
            for qa, acc_ref in ((q_aug[0], acc1_ref), (q_aug[1], acc2_ref)):
                e = _dot(qa, kt_aug)
                if masked:
                    e = jnp.where(causal_keep(), e, NEG_BIG)
                acc_ref[...] += _dot(jnp.exp2(e).astype(BF16), v_ones)

        run(step)

    @pl.when(jnp.logical_not(bounded))
    def _():
        for m_ref in (m1_ref, m2_ref):
            m_ref[...] = jnp.full(m_ref.shape, NEG_BIG, F32)

        def step(kj, masked):
            kt, v_ones, colb = tiles(kj)
            for qm, m_ref, acc_ref in ((q1, m1_ref, acc1_ref), (q2, m2_ref, acc2_ref)):
                s = _dot(qm, kt) + colb
                if masked:
                    s = jnp.where(causal_keep(), s, NEG_BIG)
                _softmax_step(s, v_ones, m_ref, acc_ref)

        run(step)

    lam = _lambda_from(lam_ref, lam_init)
    a1 = acc1_ref[...]
    a2 = acc2_ref[...]
    o = a1[:, :DK_A] / a1[:, DK_A:DK_A + 1] - lam * (a2[:, :DK_A] / a2[:, DK_A:DK_A + 1])
    o = o * lax.rsqrt(jnp.mean(o * o, axis=-1, keepdims=True) + EPS)
    o_ref[...] = (o * sg_ref[...] * (1.0 - lam_init)).astype(o_ref.dtype)


def _attn_prompt(lamv, qb, kbt, vb, sg, batch, seq, lam_init):
    n_heads = qb.shape[1] // DK_A
    tq = _largest_tile(seq, ATTN_TILE, LANES)
    nq = seq // tq
    q_spec = pl.BlockSpec((tq, DK_A), lambda b, h, i: (b * nq + i, h))
    kt_spec = pl.BlockSpec((DK_A, seq), lambda b, h, i: (h, b))
    kv_spec = pl.BlockSpec((seq, DK_A), lambda b, h, i: (b, h))
    return pl.pallas_call(
        functools.partial(_attn_prompt_kernel, tq=tq, lam_init=lam_init, n_heads=n_heads),
        out_shape=jax.ShapeDtypeStruct(qb.shape, BF16),
        grid=(batch, n_heads, nq),
        in_specs=[pl.BlockSpec(lamv.shape, lambda b, h, i: (0, 0)), q_spec, kt_spec, kv_spec,
                  pl.BlockSpec(sg.shape, lambda b, h, i: (0, 0))],
        out_specs=q_spec,
        scratch_shapes=[pltpu.VMEM((2, 1, 1), F32), pltpu.VMEM((tq, 1), F32), pltpu.VMEM((tq, 1), F32),
                        pltpu.VMEM((tq, 2 * DK_A), F32), pltpu.VMEM((tq, 2 * DK_A), F32)],
        compiler_params=_cparams(("arbitrary",) * 3),
        name="attn_prompt",
    )(lamv, qb, kbt, vb, sg)


def _attn_sample_kernel(pt_ref, lam_ref, q_ref, *rest, pages, page, past, t_dec, n_heads, lam_init):
    k_refs = rest[:pages]
    v_refs = rest[pages:2 * pages]
    kn_ref, vn_ref, sg_ref, o_ref, m_ref, acc_ref = rest[2 * pages:]
    j = pl.program_id(1)
    nrow = 2 * n_heads * t_dec
    ncol = page * n_heads

    @pl.when(j == 0)
    def _():
        m_ref[...] = jnp.full(m_ref.shape, NEG_BIG, F32)
        acc_ref[...] = jnp.zeros(acc_ref.shape, F32)

    q = q_ref[...]
    r = lax.broadcasted_iota(jnp.int32, (nrow, 1), 0)
    hrow = (r % (n_heads * t_dec)) // t_dec
    trow = r % t_dec
    slope2 = _alibi_slope(hrow, n_heads) * LOG2E
    col = lax.broadcasted_iota(jnp.int32, (1, ncol), 1)
    tok = col // n_heads
    same_head = (col % n_heads) == hrow
    base = jnp.where(same_head, slope2 * tok.astype(F32), NEG_BIG)

    for p in range(pages):
        off = slope2 * ((j * pages + p) * page - past).astype(F32)
        s = _dot_nt(q, k_refs[p][...].astype(BF16)) + (base + off)
        _softmax_step(s, _with_ones(v_refs[p][...].astype(BF16)), m_ref, acc_ref)

    @pl.when(j == pl.num_programs(1) - 1)
    def _():
        nnew = kn_ref.shape[0]
        ok = (tok[:, :nnew] <= trow) & (col[:, :nnew] < n_heads * t_dec)
        s = jnp.where(ok, _dot_nt(q, kn_ref[...]) + base[:, :nnew], NEG_BIG)
        _softmax_step(s, _with_ones(vn_ref[...]), m_ref, acc_ref)
        lam = _lambda_from(lam_ref, lam_init)
        acc = acc_ref[...]
        o = acc[:, :DK_A] / acc[:, DK_A:DK_A + 1]
        half = nrow // 2
        o = o[0:half] - lam * o[half:nrow]
        o = o * lax.rsqrt(jnp.mean(o * o, axis=-1, keepdims=True) + EPS)
        o_ref[...] = o * sg_ref[...] * (1.0 - lam_init)


def _attn_sample(pt_flat, lamv, q_rows, cache_kr, cache_vr, kn, vn, sg, n_pages, page, t_dec, past,
                 lam_init):
    bs, nrow, _ = q_rows.shape
    n_heads = nrow // (2 * t_dec)
    prow = page * n_heads
    pages = _largest_tile(n_pages, 16, 1)
    steps = n_pages // pages

    def page_spec(p):
        return pl.BlockSpec((prow, DK_A), lambda b, j, pt: (pt[b * n_pages + j * pages + p], 0))

    per_b = lambda a: pl.BlockSpec((None,) + a.shape[1:], lambda b, j, pt: (b, 0, 0))
    const = lambda a: pl.BlockSpec(a.shape, lambda b, j, pt: (0, 0))
    grid_spec = pltpu.PrefetchScalarGridSpec(
        num_scalar_prefetch=1,
        grid=(bs, steps),
        in_specs=[const(lamv), per_b(q_rows)] + [page_spec(p) for p in range(pages)]
                 + [page_spec(p) for p in range(pages)] + [per_b(kn), per_b(vn), const(sg)],
        out_specs=pl.BlockSpec((None, nrow // 2, DK_A), lambda b, j, pt: (b, 0, 0)),
        scratch_shapes=[pltpu.VMEM((nrow, 1), F32), pltpu.VMEM((nrow, 2 * DK_A), F32)],
    )
    return pl.pallas_call(
        functools.partial(_attn_sample_kernel, pages=pages, page=page, past=past, t_dec=t_dec,
                          n_heads=n_heads, lam_init=lam_init),
        out_shape=jax.ShapeDtypeStruct((bs, nrow // 2, DK_A), F32),
        grid_spec=grid_spec,
        compiler_params=_cparams(("arbitrary", "arbitrary")),
        name="attn_sample",
    )(pt_flat, lamv, q_rows, *([cache_kr] * pages), *([cache_vr] * pages), kn, vn, sg)


CONV_B_HALO = 32


def _convb_kernel(u_ref, buf_ref, w_ref, b_ref, lg_ref, lb_ref, cv_ref, nb_ref, xp_ref, xs_ref, *, tm):
    t = pl.program_id(1)
    pad = CONV_B_HALO - (CONV_B_WIDTH - 1)
    shifted_rows = tm + CONV_B_HALO - SUBLANES

    @pl.when(t == 0)
    def _():
        xp_ref[0:CONV_B_HALO] = buf_ref[...]

    @pl.when(t > 0)
    def _():
        xp_ref[0:CONV_B_HALO] = xp_ref[tm:tm + CONV_B_HALO]

    xp_ref[CONV_B_HALO:CONV_B_HALO + tm] = u_ref[...]
    for s in range(1, SUBLANES):
        xs_ref[s - 1, 0:shifted_rows] = xp_ref[s:s + shifted_rows]
    ch = min(64, tm)
    for c in range(tm // ch):
        acc = jnp.zeros((ch, u_ref.shape[1]), F32)
        for j in range(CONV_B_WIDTH):
            s = (pad + j) % SUBLANES
            s0 = c * ch + (pad + j) - s
            src = xp_ref if s == 0 else xs_ref.at[s - 1]
            acc = acc + w_ref[j:j + 1, :] * src[s0:s0 + ch, :]
        y = acc + b_ref[...]
        mu = jnp.mean(y, axis=-1, keepdims=True)
        yc = y - mu
        z = yc * lax.rsqrt(jnp.mean(yc * yc, axis=-1, keepdims=True) + EPS) * lg_ref[...] + lb_ref[...]
        cv_ref[c * ch:(c + 1) * ch] = (z * jax.nn.sigmoid(z)).astype(cv_ref.dtype)
    nb_ref[...] = xp_ref[tm:tm + CONV_B_HALO]


def _convb(u3, buf_pad, w_pad, b, lg, lb, out_dtype):
    nb, t, c = u3.shape
    tm = t if t < SUBLANES else _largest_tile(t, 256)
    nt = t // tm
    const = lambda a: pl.BlockSpec(a.shape, lambda bb, tt: (0, 0))
    return pl.pallas_call(
        functools.partial(_convb_kernel, tm=tm),
        out_shape=[jax.ShapeDtypeStruct((nb, t, c), out_dtype),
                   jax.ShapeDtypeStruct((nb, CONV_B_HALO, c), F32)],
        grid=(nb, nt),
        in_specs=[pl.BlockSpec((None, tm, c), lambda bb, tt: (bb, tt, 0)),
                  pl.BlockSpec((None, CONV_B_HALO, c), lambda bb, tt: (bb, 0, 0)),
                  const(w_pad), const(b), const(lg), const(lb)],
        out_specs=[pl.BlockSpec((None, tm, c), lambda bb, tt: (bb, tt, 0)),
                   pl.BlockSpec((None, CONV_B_HALO, c), lambda bb, tt: (bb, 0, 0))],
        scratch_shapes=[pltpu.VMEM((tm + CONV_B_HALO, c), F32),
                        pltpu.VMEM((SUBLANES - 1, -(-tm // SUBLANES) * SUBLANES + CONV_B_HALO - SUBLANES, c), F32)],
        compiler_params=_cparams(("arbitrary", "arbitrary")),
        name="conv_b",
    )(u3, buf_pad, w_pad, b, lg, lb)


def _outproj_even_kernel(o_ref, cv_ref, w_ref, x_ref, g1_ref, gf_ref, sc2_ref, sh2_ref,
                         x1_ref, h2_ref, *, aw):
    mix = (_dot(o_ref[...].astype(BF16), w_ref[0:aw, :])
           + _dot(cv_ref[...].astype(BF16), w_ref[aw:, :]))
    x1 = x_ref[...] + g1_ref[...] * mix
    x1_ref[...] = x1
    h2_ref[...] = _modnorm(x1, gf_ref[...], sc2_ref[...], sh2_ref[...])


def _outproj_even(o, cv, w_bf, x, tm, g1, gf, sc2, sh2, mod_map):
    rows, aw = o.shape
    d = x.shape[1]
    mr = g1.shape[1]
    const = lambda a: pl.BlockSpec(a.shape, lambda i: (0,) * a.ndim)
    mod_spec = pl.BlockSpec((None, mr, d), lambda i: (mod_map(i), 0, 0))
    xrow = pl.BlockSpec((tm, d), lambda i: (i, 0))
    return pl.pallas_call(
        functools.partial(_outproj_even_kernel, aw=aw),
        out_shape=[jax.ShapeDtypeStruct((rows, d), F32)] * 2,
        grid=(rows // tm,),
        in_specs=[pl.BlockSpec((tm, aw), lambda i: (i, 0)),
                  pl.BlockSpec((tm, cv.shape[1]), lambda i: (i, 0)),
                  const(w_bf), xrow, mod_spec, const(gf), mod_spec, mod_spec],
        out_specs=[xrow, xrow],
        compiler_params=_cparams(("arbitrary",)),
        name="outproj_even",
    )(o, cv, w_bf, x, g1, gf, sc2, sh2)


CONV_C_HALO = 8


def _odd_kernel(x_ref, gm_ref, sc1_ref, sh1_ref, win_ref, cw_ref, halo_ref, wout_ref, g1_ref,
                gf_ref, sc2_ref, sh2_ref, x1_ref, h2_ref, nb_ref, zp_ref,
                *, tm, cwid, seq_tiled, t_dec):
    x = x_ref[...]
    h = _modnorm(x, gm_ref[...], sc1_ref[...], sh1_ref[...]).astype(BF16)
    cg = _dot(h, win_ref[:, cwid:2 * cwid])
    hx = _dot(h, win_ref[:, 2 * cwid:3 * cwid])
    z = cg * hx
    halo = CONV_C_HALO
    if seq_tiled:
        t = pl.program_id(1)

        @pl.when(t == 0)
        def _():
            zp_ref[0:halo] = halo_ref[...]

        @pl.when(t > 0)
        def _():
            zp_ref[0:halo] = zp_ref[tm:tm + halo]
    else:
        zp_ref[0:halo] = jnp.zeros((halo, cwid), F32)
    zp_ref[halo:halo + tm] = z
    prev1 = zp_ref[halo - 1:halo - 1 + tm]
    prev2 = zp_ref[halo - 2:halo - 2 + tm]
    if seq_tiled:
        nb_ref[...] = zp_ref[tm:tm + halo]
    else:
        tpos = lax.broadcasted_iota(jnp.int32, (tm, 1), 0) % t_dec
        prev1 = jnp.where(tpos >= 1, prev1, halo_ref[0])
        prev2 = jnp.where(tpos >= 2, prev2, halo_ref[1])
        nb_ref[...] = z
    y = cw_ref[0:1, :] * prev2 + cw_ref[1:2, :] * prev1 + cw_ref[2:3, :] * z
    bg = _dot(h, win_ref[:, 0:cwid])
    mix = _dot((bg * y).astype(BF16), wout_ref[...])
    x1 = x + g1_ref[...] * mix
    x1_ref[...] = x1
    h2_ref[...] = _modnorm(x1, gf_ref[...], sc2_ref[...], sh2_ref[...])


def _odd_mixer(x, nb, t, tm, gm, sc1, sh1, mod_map, win_bf, cw, halo, wout_bf, g1,
               gf, sc2, sh2, seq_tiled, t_dec):
    rows, d = x.shape
    cwid = wout_bf.shape[0]
    mr = g1.shape[1]
    if seq_tiled:
        nt = t // tm
        grid = (nb, nt)
        xmap = lambda b, i: (b * nt + i, 0)
        halo_spec = pl.BlockSpec((None, CONV_C_HALO, cwid), lambda b, i: (b, 0, 0))
        nb_shape = jax.ShapeDtypeStruct((nb, CONV_C_HALO, cwid), F32)
        nb_spec = pl.BlockSpec((None, CONV_C_HALO, cwid), lambda b, i: (b, 0, 0))
        mmap = lambda b, i: (mod_map(b, i), 0, 0)
    else:
        grid = (1, 1)
        xmap = lambda b, i: (0, 0)
        halo_spec = pl.BlockSpec(halo.shape, lambda b, i: (0, 0, 0))
        nb_shape = jax.ShapeDtypeStruct((tm, cwid), F32)
        nb_spec = pl.BlockSpec((tm, cwid), lambda b, i: (0, 0))
        mmap = lambda b, i: (0, 0, 0)
    const = lambda a: pl.BlockSpec(a.shape, lambda b, i: (0,) * a.ndim)
    mod_spec = pl.BlockSpec((None, mr, d), mmap)
    xrow = pl.BlockSpec((tm, d), xmap)
    return pl.pallas_call(
        functools.partial(_odd_kernel, tm=tm, cwid=cwid, seq_tiled=seq_tiled, t_dec=t_dec),
        out_shape=[jax.ShapeDtypeStruct((rows, d), F32)] * 2 + [nb_shape],
        grid=grid,
        in_specs=[xrow, const(gm), mod_spec, mod_spec, const(win_bf), const(cw), halo_spec,
                  const(wout_bf), mod_spec, const(gf), mod_spec, mod_spec],
        out_specs=[xrow, xrow, nb_spec],
        scratch_shapes=[pltpu.VMEM((tm + CONV_C_HALO, cwid), F32)],
        compiler_params=_cparams(("arbitrary", "arbitrary")),
        name="odd_mixer",
    )(x, gm, sc1, sh1, win_bf, cw, halo, wout_bf, g1, gf, sc2, sh2)


def _pack_bf16_pairs(a):
    half = a.shape[1] // 2
    lo = lax.bitcast_convert_type(a[:, :half].astype(F32), jnp.uint32) >> 16
    hi = lax.bitcast_convert_type(a[:, half:].astype(F32), jnp.uint32) & jnp.uint32(0xFFFF0000)
    return lo | hi


def _unpack_bf16_pairs(w):
    lo = lax.bitcast_convert_type(w << 16, F32).astype(BF16)
    hi = lax.bitcast_convert_type(w & jnp.uint32(0xFFFF0000), F32).astype(BF16)
    return lo, hi


def _route_kernel(h_ref, w01_ref, w0_ref, b_ref, tri_ref, cnt0_ref, meta_ref, cnt_ref, hp_ref,
                  carry_ref):
    i = pl.program_id(0)

    @pl.when(i == 0)
    def _():
        carry_ref[...] = cnt0_ref[...]

    h = h_ref[...]
    a0 = h.astype(BF16)
    a1 = (h - a0.astype(F32)).astype(BF16)
    t01 = _dot(a0, w01_ref[...])
    logits = t01[:, :LANES] + t01[:, LANES:] + _dot(a1, w0_ref[...]) + b_ref[...]
    hp_ref[...] = _pack_bf16_pairs(a0)
    tm = logits.shape[0]
    lane = lax.broadcasted_iota(jnp.int32, logits.shape, 1)
    lane_f = lane.astype(F32)
    neg = jnp.float32(-jnp.inf)

    def first_max(vals):
        vmax = jnp.max(vals, axis=-1, keepdims=True)
        idx = jnp.min(jnp.where(vals == vmax, lane_f, float(LANES)), axis=-1, keepdims=True)
        return vmax, idx.astype(jnp.int32)

    gmask = lane < N_GROUPS
    gmax, g_idx = first_max(jnp.where(gmask, logits, neg))
    g_w = 1.0 / jnp.sum(jnp.where(gmask, jnp.exp(logits - gmax), 0.0), axis=-1, keepdims=True)
    lo = N_GROUPS + EXPERTS_PER_GROUP * g_idx
    el = jnp.where((lane >= lo) & (lane < lo + EXPERTS_PER_GROUP), logits, neg)
    v1, i1 = first_max(el)
    v2, i2 = first_max(jnp.where(lane == i1, neg, el))
    tt = jnp.exp(v2 - v1)
    wgt1 = g_w / (1.0 + tt)
    wgt2 = g_w * tt / (1.0 + tt)
    e1 = i1 - N_GROUPS
    e2 = i2 - N_GROUPS
    hit1 = lane == e1
    hit2 = lane == e2
    onehot = (hit1 | hit2).astype(F32)
    cnt = _dot(tri_ref[...], onehot.astype(BF16)) + carry_ref[...]
    r1 = jnp.sum(jnp.where(hit1, cnt, 0.0), axis=-1, keepdims=True)
    r2 = jnp.sum(jnp.where(hit2, cnt, 0.0), axis=-1, keepdims=True)
    carry_ref[...] = carry_ref[...] + jnp.sum(onehot, axis=0, keepdims=True)
    cnt_ref[...] = carry_ref[...]
    cols = (e1.astype(F32), e2.astype(F32), r1, r2, wgt1, wgt2)
    meta = jnp.zeros((tm, LANES), F32)
    for c, val in enumerate(cols):
        meta = jnp.where(lane == c, val, meta)
    meta_ref[...] = meta


def _route(h2, w01, w0, b_pad, cnt0, tm):
    n, d = h2.shape
    tri = (jnp.arange(tm)[:, None] > jnp.arange(tm)[None, :]).astype(BF16)
    const = lambda a: pl.BlockSpec(a.shape, lambda i: (0, 0))
    return pl.pallas_call(
        _route_kernel,
        out_shape=[jax.ShapeDtypeStruct((n, LANES), F32), jax.ShapeDtypeStruct((1, LANES), F32),
                   jax.ShapeDtypeStruct((n, d // 2), jnp.uint32)],
        grid=(n // tm,),
        in_specs=[pl.BlockSpec((tm, d), lambda i: (i, 0)), const(w01), const(w0), const(b_pad),
                  const(tri), const(cnt0)],
        out_specs=[pl.BlockSpec((tm, LANES), lambda i: (i, 0)), const(cnt0),
                   pl.BlockSpec((tm, d // 2), lambda i: (i, 0))],
        scratch_shapes=[pltpu.VMEM((1, LANES), F32)],
        compiler_params=_cparams(("arbitrary",)),
        name="moe_route",
    )(h2, w01, w0, b_pad, tri, cnt0)


def _dispatch_kernel(dest_ref, h_ref, xp_in_ref, xp_ref, sem, *, tm):
    del xp_in_ref

    def row_copy(r, slot):
        return pltpu.make_async_copy(h_ref.at[pl.ds(r, 1)], xp_ref.at[pl.ds(slot, 1)], sem)

    def issue(r, c):
        for k in range(TOP_K):
            row_copy(r, dest_ref[0, TOP_K * r + k]).start()
        return c

    lax.fori_loop(0, tm, issue, 0, unroll=DMA_UNROLL)

    def drain(r, c):
        for k in range(TOP_K):
            row_copy(r, dest_ref[0, TOP_K * r + k]).wait()
        return c

    lax.fori_loop(0, tm, drain, 0, unroll=DMA_UNROLL)


def _dispatch(dest3, h2_all, xp_zero, tm):
    n, d = h2_all.shape
    return pl.pallas_call(
        functools.partial(_dispatch_kernel, tm=tm),
        out_shape=jax.ShapeDtypeStruct(xp_zero.shape, xp_zero.dtype),
        grid=(n // tm,),
        in_specs=[pl.BlockSpec((None, 1, TOP_K * tm), lambda i: (i, 0, 0), memory_space=pltpu.SMEM),
                  pl.BlockSpec((tm, d), lambda i: (i, 0)),
                  pl.BlockSpec(memory_space=pl.ANY)],
        out_specs=pl.BlockSpec(memory_space=pl.ANY),
        input_output_aliases={2: 0},
        scratch_shapes=[pltpu.SemaphoreType.DMA],
        compiler_params=_cparams(("arbitrary",)),
        name="moe_dispatch",
    )(dest3, h2_all, xp_zero)


def _expert_kernel(be_ref, nu_ref, x_ref, wgu_ref, wd_ref, y_ref, wgu_bf, wd_bf, *, f, n_blocks):
    s = pl.program_id(0)
    b = s - 1

    @pl.when((b >= 0) & (b < nu_ref[0]))
    def _():
        x_lo, x_hi = _unpack_bf16_pairs(x_ref[...])
        half = x_lo.shape[1]
        gate = _dot(x_lo, wgu_bf[0:half, 0:f]) + _dot(x_hi, wgu_bf[half:, 0:f])
        up = _dot(x_lo, wgu_bf[0:half, f:2 * f]) + _dot(x_hi, wgu_bf[half:, f:2 * f])
        act = (gate * jax.nn.sigmoid(gate) * up).astype(BF16)
        y_ref[...] = _dot(act, wd_bf[...])

    @pl.when(b >= nu_ref[0])
    def _():
        y_ref[...] = jnp.zeros(y_ref.shape, F32)

    nxt = be_ref[jnp.minimum(s, n_blocks - 1)]

    @pl.when((s == 0) | (nxt != be_ref[jnp.maximum(b, 0)]))
    def _():
        wgu_bf[...] = wgu_ref[...].astype(BF16)
        wd_bf[...] = wd_ref[...].astype(BF16)


def _experts(block_e, n_used, xp, w_gate_up, w_down, layer):
    n_slots, dh = xp.shape
    d = 2 * dh
    f = w_down.shape[2]
    n_blocks = n_slots // MOE_ROWS
    blk = lambda s: jnp.maximum(s - 1, 0)
    nxt = lambda s, be: be[jnp.minimum(s, n_blocks - 1)]
    grid_spec = pltpu.PrefetchScalarGridSpec(
        num_scalar_prefetch=2,
        grid=(n_blocks + 1,),
        in_specs=[pl.BlockSpec((MOE_ROWS, dh), lambda s, be, nu: (blk(s), 0)),
                  pl.BlockSpec((None, None, d, 2 * f), lambda s, be, nu: (layer, nxt(s, be), 0, 0)),
                  pl.BlockSpec((None, None, f, d), lambda s, be, nu: (layer, nxt(s, be), 0, 0))],
        out_specs=pl.BlockSpec((MOE_ROWS, d), lambda s, be, nu: (blk(s), 0)),
        scratch_shapes=[pltpu.VMEM((d, 2 * f), BF16), pltpu.VMEM((f, d), BF16)],
    )
    return pl.pallas_call(
        functools.partial(_expert_kernel, f=f, n_blocks=n_blocks),
        out_shape=jax.ShapeDtypeStruct((n_slots, d), F32),
        grid_spec=grid_spec,
        compiler_params=_cparams(("arbitrary",)),
        name="moe_experts",
    )(block_e, n_used, xp, w_gate_up, w_down)


def _combine_kernel(dest_ref, x_ref, meta_ref, g2_ref, yp_ref, o_ref, ybuf, sem, *, tm):

    def row_copy(r, k, slot):
        return pltpu.make_async_copy(yp_ref.at[pl.ds(slot, 1)], ybuf.at[k, pl.ds(r, 1)], sem)

    def issue(r, c):
        for k in range(TOP_K):
            row_copy(r, k, dest_ref[0, TOP_K * r + k]).start()
        return c

    lax.fori_loop(0, tm, issue, 0, unroll=DMA_UNROLL)

    def drain(r, c):
        for k in range(TOP_K):
            row_copy(r, k, dest_ref[0, TOP_K * r + k]).wait()
        return c

    lax.fori_loop(0, tm, drain, 0, unroll=DMA_UNROLL)
    meta = meta_ref[...]
    y = meta[:, 4:5] * ybuf[0] + meta[:, 5:6] * ybuf[1]
    o_ref[...] = x_ref[...] + g2_ref[...] * y


def _combine(dest3, x1, meta, g2, mod_map, yp, tm):
    rows, d = x1.shape
    mr = g2.shape[1]
    xrow = pl.BlockSpec((tm, d), lambda i: (i, 0))
    return pl.pallas_call(
        functools.partial(_combine_kernel, tm=tm),
        out_shape=jax.ShapeDtypeStruct((rows, d), F32),
        grid=(rows // tm,),
        in_specs=[pl.BlockSpec((None, 1, TOP_K * tm), lambda i: (i, 0, 0), memory_space=pltpu.SMEM),
                  xrow, pl.BlockSpec((tm, LANES), lambda i: (i, 0)),
                  pl.BlockSpec((None, mr, d), lambda i: (mod_map(i), 0, 0)),
                  pl.BlockSpec(memory_space=pl.ANY)],
        out_specs=xrow,
        scratch_shapes=[pltpu.VMEM((TOP_K, tm, d), F32), pltpu.SemaphoreType.DMA],
        compiler_params=_cparams(("arbitrary",)),
        name="moe_combine",
    )(dest3, x1, meta, g2, yp)


def _moe(h2_p, h2_s, x1_p, x1_s, g2_p, g2_s, w01, w0, b_pad, w_gate_up, w_down, layer, tm_p, seq):
    n_p, d = h2_p.shape
    n_s = h2_s.shape[0]
    n = n_p + n_s
    meta_p, cnt_p, hp_p = _route(h2_p, w01, w0, b_pad, jnp.zeros((1, LANES), F32), tm_p)
    meta_s, counts, hp_s = _route(h2_s, w01, w0, b_pad, cnt_p, n_s)
    sizes = counts[0, :N_EXPERTS].astype(jnp.int32)
    padded = ((sizes + MOE_ROWS - 1) // MOE_ROWS) * MOE_ROWS
    pad_end = jnp.cumsum(padded)
    pad_start = pad_end - padded

    def slots(meta):
        e_tok = meta[:, 0:TOP_K].astype(jnp.int32)
        rank = meta[:, TOP_K:2 * TOP_K].astype(jnp.int32)
        start = jnp.sum(jnp.where(e_tok[:, :, None] == jnp.arange(N_EXPERTS)[None, None, :],
                                  pad_start[None, None, :], 0), axis=-1)
        return (start + rank).astype(jnp.int32)

    dest_p = slots(meta_p).reshape(n_p // tm_p, 1, TOP_K * tm_p)
    dest_s = slots(meta_s).reshape(1, 1, TOP_K * n_s)
    n_blocks = (n * TOP_K + MOE_ROWS - 1) // MOE_ROWS + N_EXPERTS
    blk0 = jnp.arange(n_blocks, dtype=jnp.int32) * MOE_ROWS
    block_e = jnp.minimum(jnp.sum(pad_end[None, :] <= blk0[:, None], axis=-1), N_EXPERTS - 1).astype(jnp.int32)
    n_used = (pad_end[-1:] // MOE_ROWS).astype(jnp.int32)

    xp = _dispatch(dest_p, hp_p, jnp.zeros((n_blocks * MOE_ROWS, d // 2), jnp.uint32), tm_p)
    xp = _dispatch(dest_s, hp_s, xp, n_s)
    yp = _experts(block_e, n_used, xp, w_gate_up, w_down, layer)
    per_seq = seq // tm_p
    out_p = _combine(dest_p, x1_p, meta_p, g2_p, lambda i: i // per_seq, yp, tm_p)
    out_s = _combine(dest_s, x1_s, meta_s, g2_s, lambda i: 0, yp, n_s)
    return out_p, out_s


def kernel(x_prompt, x_sample, cache_k, cache_v, state_conv_b, state_conv_c, page_table, c_prompt, c_sample, norm_mix_g, norm_ffn_g, w_ada, b_ada, w_in_a, q_norm_g, k_norm_g, lam_q1, lam_k1, lam_q2, lam_k2, subln_g, convb_w, convb_b, convb_ln_g, convb_ln_b, w_out_a, w_in_c, convc_w, w_out_c, w_group, b_group, w_router, b_router, w_gate_up, w_down):
    bp, seq, d = x_prompt.shape
    bs, t_dec, _ = x_sample.shape
    depth = w_ada.shape[0]
    n_p, n_s = bp * seq, bs * t_dec
    n = n_p + n_s
    n_pool, page = cache_k.shape[1], cache_k.shape[2]
    n_heads = cache_k.shape[3]
    aw = n_heads * DK_A
    bw = state_conv_b.shape[-1]
    cwid = state_conv_c.shape[-1]
    n_pages = page_table.shape[1]
    past = n_pages * page
    tm_p = _largest_tile(seq, 512, LANES)
    assert n_s % SUBLANES == 0

    x_p = x_prompt.reshape(n_p, d)
    x_s = x_sample.reshape(n_s, d)

    c_all = jnp.concatenate([c_prompt, c_sample], axis=0)
    rows = c_all.shape[0]
    rows_pad = -(-rows // SUBLANES) * SUBLANES
    mods = _ada_all(jnp.pad(c_all, ((0, rows_pad - rows), (0, 0))), w_ada, b_ada)
    mods = mods[:, :rows].reshape(depth, rows, 6, d).transpose(0, 2, 1, 3)
    mods_p = mods[:, :, :bp, None, :]
    mods_s = jnp.repeat(mods[:, :, bp:], t_dec, axis=2)[:, :, None]

    def mod_rows(l, j):
        return mods_p[l, j], mods_s[l, j]

    bd = (jnp.arange(aw)[:, None] // HD_A == jnp.arange(aw)[None, :] // HD_A).astype(BF16)
    cache_kr = cache_k.reshape(-1, DK_A)
    cache_vr = cache_v.reshape(-1, DK_A)
    per_seq = seq // tm_p
    pmap1 = lambda i: i // per_seq
    smap1 = lambda i: 0
    row2 = lambda a: a.reshape(1, -1)

    k_p, v_p, k_s, v_s, cb_p, cb_s, cc_p, cc_s = [], [], [], [], [], [], [], []
    for l in range(depth):
        i = l // 2
        sh1_p, sh1_s = mod_rows(l, 0)
        sc1_p, sc1_s = mod_rows(l, 1)
        g1_p, g1_s = mod_rows(l, 2)
        sh2_p, sh2_s = mod_rows(l, 3)
        sc2_p, sc2_s = mod_rows(l, 4)
        g2_p, g2_s = mod_rows(l, 5)
        gm = row2(norm_mix_g[l])
        gf = row2(norm_ffn_g[l])
        if l % 2 == 0:
            lam_init = 0.8 - 0.6 * math.exp(-0.3 * l)
            w_in = w_in_a[i].astype(BF16)
            w_out = w_out_a[i].astype(BF16)
            qg = jnp.tile(q_norm_g[i].reshape(1, DK_A), (1, n_heads))
            kg = jnp.tile(k_norm_g[i].reshape(1, DK_A), (1, n_heads))
            lamv = jnp.stack([lam_q1[i], lam_k1[i], lam_q2[i], lam_k2[i]])
            sg = row2(subln_g[i])
            cw = jnp.pad(convb_w[i].reshape(CONV_B_WIDTH, bw), ((0, 1), (0, 0)))
            cbb, clg, clb = row2(convb_b[i]), row2(convb_ln_g[i]), row2(convb_ln_b[i])
            halo_pad = ((0, 0), (CONV_B_HALO - (CONV_B_WIDTH - 1), 0), (0, 0))

            qb, kf, kbt, vf, vb, u = _inproj_even(x_p, tm_p, gm, sc1_p, sh1_p, pmap1,
                                                  w_in, bd, qg, kg, aw, bw, True)
            o_p = _attn_prompt(lamv, qb, kbt, vb, sg, bp, seq, lam_init)
            cv_p, nb_p = _convb(u.reshape(bp, seq, bw), jnp.zeros((bp, CONV_B_HALO, bw), F32),
                                cw, cbb, clg, clb, BF16)
            x1_p, h2_p = _outproj_even(o_p, cv_p.reshape(n_p, bw), w_out, x_p, tm_p,
                                       g1_p, gf, sc2_p, sh2_p, pmap1)
            k_p.append(kf.reshape(bp, seq, n_heads, DK_A))
            v_p.append(vf.reshape(bp, seq, n_heads, DK_A))
            cb_p.append(nb_p[:, CONV_B_HALO - (CONV_B_WIDTH - 1):])

            qb, kf, kb, vf, vb, u = _inproj_even(x_s, n_s, gm, sc1_s, sh1_s, smap1,
                                                 w_in, bd, qg, kg, aw, bw, False)
            q4 = qb.reshape(bs, t_dec, n_heads, 2, HD_A).transpose(0, 2, 1, 3, 4)[:, None]
            eye_m = jnp.eye(2, dtype=BF16)[None, :, None, None, :, None]
            q_rows = (q4 * eye_m).reshape(bs, 2 * n_heads * t_dec, DK_A)
            new_pad = ((0, 0), (0, LANES - t_dec * n_heads), (0, 0))
            kn = jnp.pad(kb.reshape(bs, t_dec * n_heads, DK_A), new_pad)
            vn = jnp.pad(vb.reshape(bs, t_dec * n_heads, DK_A), new_pad)
            pt_flat = (page_table.astype(jnp.int32) + i * n_pool).reshape(-1)
            o_s = _attn_sample(pt_flat, lamv, q_rows, cache_kr, cache_vr, kn, vn, sg, n_pages, page,
                               t_dec, past, lam_init)
            o_s = o_s.reshape(bs, n_heads, t_dec, DK_A).transpose(0, 2, 1, 3)
            cv_s, nb_s = _convb(u.reshape(bs, t_dec, bw), jnp.pad(state_conv_b[i], halo_pad),
                                cw, cbb, clg, clb, F32)
            x1_s, h2_s = _outproj_even(o_s.reshape(n_s, aw), cv_s.reshape(n_s, bw), w_out,
                                       x_s, n_s, g1_s, gf, sc2_s, sh2_s, smap1)
            k_s.append(kf.reshape(bs, t_dec, n_heads, DK_A))
            v_s.append(vf.reshape(bs, t_dec, n_heads, DK_A))
            cb_s.append(nb_s[:, CONV_B_HALO - (CONV_B_WIDTH - 1):])
        else:
            w_in = w_in_c[i].astype(BF16)
            w_out = w_out_c[i].astype(BF16)
            cw = jnp.pad(convc_w[i].reshape(CONV_C_WIDTH, cwid), ((0, SUBLANES - CONV_C_WIDTH), (0, 0)))
            halo_p = jnp.zeros((bp, CONV_C_HALO, cwid), F32)
            x1_p, h2_p, nb_p = _odd_mixer(
                x_p, bp, seq, tm_p, gm, sc1_p, sh1_p, lambda b, t: b, w_in, cw, halo_p, w_out,
                g1_p, gf, sc2_p, sh2_p, True, t_dec)
            cc_p.append(nb_p[:, CONV_C_HALO - (CONV_C_WIDTH - 1):])
            st = state_conv_c[i]
            zrow = jnp.zeros((bs, 1, cwid), F32)
            halo1 = jnp.concatenate([st[:, 1:2]] + [zrow] * (t_dec - 1), axis=1).reshape(n_s, cwid)
            halo2 = jnp.concatenate([st[:, 0:1], st[:, 1:2]] + [zrow] * (t_dec - 2), axis=1).reshape(n_s, cwid)
            x1_s, h2_s, z_s = _odd_mixer(
                x_s, 1, n_s, n_s, gm, sc1_s, sh1_s, None, w_in, cw, jnp.stack([halo1, halo2]),
                w_out, g1_s, gf, sc2_s, sh2_s, False, t_dec)
            cc_s.append(z_s.reshape(bs, t_dec, cwid)[:, t_dec - (CONV_C_WIDTH - 1):])

        w_r = jnp.pad(jnp.concatenate([w_group[l], w_router[l]], axis=1),
                      ((0, 0), (0, LANES - N_GROUPS - N_EXPERTS)))
        w0 = w_r.astype(BF16)
        w1 = (w_r - w0.astype(F32)).astype(BF16)
        w01 = jnp.concatenate([w0, w1], axis=1)
        b_r = jnp.pad(jnp.concatenate([b_group[l], b_router[l]]), (0, LANES - N_GROUPS - N_EXPERTS))
        x_p, x_s = _moe(h2_p, h2_s, x1_p, x1_s, g2_p, g2_s, w01, w0, row2(b_r),
                        w_gate_up, w_down, l, tm_p, seq)

    return (x_p.reshape(bp, seq, d), x_s.reshape(bs, t_dec, d),
            jnp.stack(k_p), jnp.stack(v_p), jnp.stack(k_s), jnp.stack(v_s),
            jnp.stack(cb_p), jnp.stack(cb_s), jnp.stack(cc_p), jnp.stack(cc_s))
```

```python
import functools
import math

import jax
import jax.numpy as jnp
from jax import lax
from jax.experimental import pallas as pl
from jax.experimental.pallas import tpu as pltpu

F32 = jnp.float32
BF16 = jnp.bfloat16

HD_A = 64
DK_A = 2 * HD_A
N_GROUPS = 4
EXPERTS_PER_GROUP = 8
N_EXPERTS = N_GROUPS * EXPERTS_PER_GROUP
TOP_K = 2
CONV_B_WIDTH = 31
CONV_C_WIDTH = 3
ALIBI_MAX = 8.0
EPS = 1e-6
NEG_BIG = -1e30
LOG2E = math.log2(math.e)

LANES = 128
SUBLANES = 8
MOE_ROWS = 256
STAB_LIMIT = 60.0
BOUND_SLACK = 1.01
ATTN_TILE = 512
DMA_UNROLL = 8
VMEM_LIMIT = 56 * 1024 * 1024


def _cparams(sem, vmem=VMEM_LIMIT):
    return pltpu.CompilerParams(dimension_semantics=sem, vmem_limit_bytes=vmem)


def _largest_tile(n, cap, mult=SUBLANES):
    best = None
    for t in range(mult, min(n, cap) + 1, mult):
        if n % t == 0:
            best = t
    assert best is not None, (n, cap, mult)
    return best


def _modnorm(x, g, scale, shift):
    ms = jnp.mean(x * x, axis=-1, keepdims=True)
    return (x * lax.rsqrt(ms + EPS)) * g * (1.0 + scale) + shift


def _dot(a, b):
    return jnp.dot(a, b, preferred_element_type=F32)


def _dot_nt(a, b):
    return lax.dot_general(a, b, (((1,), (1,)), ((), ())), preferred_element_type=F32)


def _split3(a):
    a0 = a.astype(BF16)
    r1 = a - a0.astype(F32)
    a1 = r1.astype(BF16)
    a2 = (r1 - a1.astype(F32)).astype(BF16)
    return a0, a1, a2


def _ada_kernel(c_ref, w_ref, b_ref, o_ref):
    c = c_ref[...]
    a = (c * jax.nn.sigmoid(c)).astype(BF16)
    o_ref[...] = _dot(a, w_ref[...].astype(BF16)) + b_ref[...]


def _ada_all(c_pad, w_ada, b_ada):
    depth, d, n6 = w_ada.shape
    rows = c_pad.shape[0]
    tn = 1536
    return pl.pallas_call(
        _ada_kernel,
        out_shape=jax.ShapeDtypeStruct((depth, rows, n6), F32),
        grid=(depth, n6 // tn),
        in_specs=[pl.BlockSpec((rows, d), lambda l, j: (0, 0)),
                  pl.BlockSpec((None, d, tn), lambda l, j: (l, 0, j)),
                  pl.BlockSpec((None, 1, tn), lambda l, j: (l, 0, j))],
        out_specs=pl.BlockSpec((None, rows, tn), lambda l, j: (l, 0, j)),
        compiler_params=_cparams(("arbitrary", "arbitrary")),
        name="ada_mod",
    )(c_pad, w_ada, b_ada.reshape(depth, 1, n6))


def _inproj_even_kernel(x_ref, g_ref, sc_ref, sh_ref, w_ref, bd_ref, qg_ref, kg_ref,
                        qb_ref, kf_ref, kb_ref, vf_ref, vb_ref, u_ref, *, aw, bw, k_transposed):
    h = _modnorm(x_ref[...], g_ref[...], sc_ref[...], sh_ref[...]).astype(BF16)
    bd = bd_ref[...]

    def group_norm(t, gain):
        tt = t * t
        hi = tt.astype(BF16)
        lo = (tt - hi.astype(F32)).astype(BF16)
        ss = _dot(hi, bd) + _dot(lo, bd)
        return t * lax.rsqrt(ss * (1.0 / HD_A) + EPS) * gain

    n_heads = aw // DK_A
    tm = x_ref.shape[0]

    def store_head_rows(ref, t):
        for hh in range(n_heads):
            ref[pl.ds(hh, tm, stride=n_heads), :] = t[:, hh * DK_A:(hh + 1) * DK_A]

    q = group_norm(_dot(h, w_ref[:, 0:aw]), qg_ref[...])
    qb_ref[...] = (q * (HD_A ** -0.5 * LOG2E)).astype(BF16)
    k = group_norm(_dot(h, w_ref[:, aw:2 * aw]), kg_ref[...])
    store_head_rows(kf_ref, k)
    kb_ref[...] = (k.T if k_transposed else k).astype(BF16)
    v = _dot(h, w_ref[:, 2 * aw:3 * aw])
    store_head_rows(vf_ref, v)
    vb_ref[...] = v.astype(BF16)
    ga = _dot(h, w_ref[:, 3 * aw:3 * aw + bw])
    gb = _dot(h, w_ref[:, 3 * aw + bw:3 * aw + 2 * bw])
    u_ref[...] = ga * jax.nn.sigmoid(gb)


def _inproj_even(x, tm, g, sc, sh, mod_map, w_bf, bd, qg, kg, aw, bw, k_transposed):
    rows, d = x.shape
    nt = rows // tm
    mr = sc.shape[1]
    n_heads = aw // DK_A
    row_spec = lambda width: pl.BlockSpec((tm, width), lambda i: (i, 0))
    head_rows = pl.BlockSpec((tm * n_heads, DK_A), lambda i: (i, 0))
    const = lambda a: pl.BlockSpec(a.shape, lambda i: (0,) * a.ndim)
    mod_spec = pl.BlockSpec((None, mr, d), lambda i: (mod_map(i), 0, 0))
    kb_shape = (aw, rows) if k_transposed else (rows, aw)
    kb_spec = pl.BlockSpec((aw, tm), lambda i: (0, i)) if k_transposed else row_spec(aw)
    outs = [jax.ShapeDtypeStruct((rows, aw), BF16), jax.ShapeDtypeStruct((rows * n_heads, DK_A), F32),
            jax.ShapeDtypeStruct(kb_shape, BF16), jax.ShapeDtypeStruct((rows * n_heads, DK_A), F32),
            jax.ShapeDtypeStruct((rows, aw), BF16), jax.ShapeDtypeStruct((rows, bw), F32)]
    return pl.pallas_call(
        functools.partial(_inproj_even_kernel, aw=aw, bw=bw, k_transposed=k_transposed),
        out_shape=outs,
        grid=(nt,),
        in_specs=[pl.BlockSpec((tm, d), lambda i: (i, 0)), const(g), mod_spec, mod_spec,
                  const(w_bf), const(bd), const(qg), const(kg)],
        out_specs=[row_spec(aw), head_rows, kb_spec, head_rows, row_spec(aw), row_spec(bw)],
        compiler_params=_cparams(("arbitrary",)),
        name="inproj_even",
    )(x, g, sc, sh, w_bf, bd, qg, kg)


def _lambda_from(lam_ref, lam_init):
    lv = lam_ref[...]
    a = jnp.sum(lv[0:1] * lv[1:2], axis=-1, keepdims=True)
    b = jnp.sum(lv[2:3] * lv[3:4], axis=-1, keepdims=True)
    return jnp.exp(a) - jnp.exp(b) + lam_init


def _alibi_slope(head, n_heads):
    slope = jnp.float32(0.0)
    for hh in range(n_heads):
        slope = jnp.where(head == hh, jnp.float32(2.0 ** (-ALIBI_MAX * (hh + 1) / n_heads)), slope)
    return slope


def _softmax_step(s, v_ones, m_ref, acc_ref):
    m_old = m_ref[...]
    m_new = jnp.maximum(m_old, jnp.max(s, axis=-1, keepdims=True))
    p = jnp.exp2(s - m_new)
    acc_ref[...] = jnp.exp2(m_old - m_new) * acc_ref[...] + _dot(p.astype(BF16), v_ones)
    m_ref[...] = m_new


def _with_ones(v):
    return jnp.concatenate([v, jnp.ones(v.shape, v.dtype)], axis=-1)


def _attn_prompt_kernel(lam_ref, q_ref, kt_ref, v_ref, sg_ref, o_ref, kmax_ref, m1_ref, m2_ref,
                        acc1_ref, acc2_ref, *, tq, lam_init, n_heads):
    h = pl.program_id(1)
    qi = pl.program_id(2)

    @pl.when(qi == 0)
    def _():
        kk = kt_ref[...].astype(F32)
        kk = kk * kk
        for m in range(2):
            n2 = jnp.sum(kk[m * HD_A:(m + 1) * HD_A], axis=0, keepdims=True)
            kmax_ref[m] = jnp.sqrt(jnp.max(n2, axis=-1, keepdims=True))

    for acc_ref in (acc1_ref, acc2_ref):
        acc_ref[...] = jnp.zeros(acc_ref.shape, F32)
    slope2 = _alibi_slope(h, n_heads) * LOG2E
    q = q_ref[...]
    lane = lax.broadcasted_iota(jnp.int32, q.shape, 1)
    zero = jnp.zeros_like(q)
    q1 = jnp.where(lane < HD_A, q, zero)
    q2 = jnp.where(lane >= HD_A, q, zero)
    col = lax.broadcasted_iota(jnp.int32, (1, tq), 1)
    bound = []
    for m, qm in enumerate((q1, q2)):
        qf = qm.astype(F32)
        qn = jnp.sqrt(jnp.sum(qf * qf, axis=-1, keepdims=True))
        bound.append(qn * kmax_ref[m] * BOUND_SLACK)
    bounded = jnp.max(jnp.maximum(bound[0], bound[1])) < STAB_LIMIT

    def tiles(kj):
        r0 = pl.multiple_of(kj * tq, tq)
        colb = slope2 * ((kj - qi) * tq + col).astype(F32)
        return kt_ref[:, pl.ds(r0, tq)], _with_ones(v_ref[pl.ds(r0, tq), :]), colb

    def causal_keep():
        return (lax.broadcasted_iota(jnp.int32, (tq, tq), 0)
                >= lax.broadcasted_iota(jnp.int32, (tq, tq), 1))

    def run(step):
        def body(kj, carry):
            step(kj, False)
            return carry

        lax.fori_loop(0, qi, body, 0)
        step(qi, True)

    @pl.when(bounded)
    def _():
        row = lax.broadcasted_iota(jnp.int32, (tq, 1), 0).astype(F32)
        lane2 = lax.broadcasted_iota(jnp.int32, (tq, LANES), 1)
        sub = lax.broadcasted_iota(jnp.int32, (2 * SUBLANES, tq), 0)
        q_aug = []
        for qm, bm in ((q1, bound[0]), (q2, bound[1])):
            rt = -(bm + slope2 * row)
            rt_hi = rt.astype(BF16).astype(F32)
            rt_lo = (rt - rt_hi).astype(BF16).astype(F32)
            ext = jnp.where(lane2 == 0, rt_hi,
                            jnp.where(lane2 == 1, rt_lo, jnp.where(lane2 < 5, 1.0, 0.0)))
            q_aug.append(jnp.concatenate([qm, ext.astype(BF16)], axis=-1))
        k_pad = jnp.zeros((LANES - 2 * SUBLANES, tq), BF16)

        def step(kj, masked):
            kt, v_ones, colb = tiles(kj)
            c0, c1, c2 = (c.astype(F32) for c in _split3(colb))
            ext_k = jnp.where(sub < 2, 1.0,
                              jnp.where(sub == 2, c0,
                                        jnp.where(sub == 3, c1, jnp.where(sub == 4, c2, 0.0))))
            kt_aug = jnp.concatenate([kt, ext_k.astype(BF16), k_pad], axis=0)
            for qa, acc_ref in ((q_aug[0], acc1_ref), (q_aug[1], acc2_ref)):
                e = _dot(qa, kt_aug)
                if masked:
                    e = jnp.where(causal_keep(), e, NEG_BIG)
                acc_ref[...] += _dot(jnp.exp2(e).astype(BF16), v_ones)

        run(step)

    @pl.when(jnp.logical_not(bounded))
    def _():
        for m_ref in (m1_ref, m2_ref):
            m_ref[...] = jnp.full(m_ref.shape, NEG_BIG, F32)

        def step(kj, masked):
            kt, v_ones, colb = tiles(kj)
            for qm, m_ref, acc_ref in ((q1, m1_ref, acc1_ref), (q2, m2_ref, acc2_ref)):
                s = _dot(qm, kt) + colb
                if masked:
                    s = jnp.where(causal_keep(), s, NEG_BIG)
                _softmax_step(s, v_ones, m_ref, acc_ref)

        run(step)

    lam = _lambda_from(lam_ref, lam_init)
    a1 = acc1_ref[...]
    a2 = acc2_ref[...]
    o = a1[:, :DK_A] / a1[:, DK_A:DK_A + 1] - lam * (a2[:, :DK_A] / a2[:, DK_A:DK_A + 1])
    o = o * lax.rsqrt(jnp.mean(o * o, axis=-1, keepdims=True) + EPS)
    o_ref[...] = (o * sg_ref[...] * (1.0 - lam_init)).astype(o_ref.dtype)


def _attn_prompt(lamv, qb, kbt, vb, sg, batch, seq, lam_init):
    n_heads = qb.shape[1] // DK_A
    tq = _largest_tile(seq, ATTN_TILE, LANES)
    nq = seq // tq
    q_spec = pl.BlockSpec((tq, DK_A), lambda b, h, i: (b * nq + i, h))
    kt_spec = pl.BlockSpec((DK_A, seq), lambda b, h, i: (h, b))
    kv_spec = pl.BlockSpec((seq, DK_A), lambda b, h, i: (b, h))
    return pl.pallas_call(
        functools.partial(_attn_prompt_kernel, tq=tq, lam_init=lam_init, n_heads=n_heads),
        out_shape=jax.ShapeDtypeStruct(qb.shape, BF16),
        grid=(batch, n_heads, nq),
        in_specs=[pl.BlockSpec(lamv.shape, lambda b, h, i: (0, 0)), q_spec, kt_spec, kv_spec,
                  pl.BlockSpec(sg.shape, lambda b, h, i: (0, 0))],
        out_specs=q_spec,
        scratch_shapes=[pltpu.VMEM((2, 1, 1), F32), pltpu.VMEM((tq, 1), F32), pltpu.VMEM((tq, 1), F32),
                        pltpu.VMEM((tq, 2 * DK_A), F32), pltpu.VMEM((tq, 2 * DK_A), F32)],
        compiler_params=_cparams(("arbitrary",) * 3),
        name="attn_prompt",
    )(lamv, qb, kbt, vb, sg)


def _attn_sample_kernel(pt_ref, lam_ref, q_ref, *rest, pages, page, past, t_dec, n_heads, lam_init):
    k_refs = rest[:pages]
    v_refs = rest[pages:2 * pages]
    kn_ref, vn_ref, sg_ref, o_ref, m_ref, acc_ref = rest[2 * pages:]
    j = pl.program_id(1)
    nrow = 2 * n_heads * t_dec
    ncol = page * n_heads

    @pl.when(j == 0)
    def _():
        m_ref[...] = jnp.full(m_ref.shape, NEG_BIG, F32)
        acc_ref[...] = jnp.zeros(acc_ref.shape, F32)

    q = q_ref[...]
    r = lax.broadcasted_iota(jnp.int32, (nrow, 1), 0)
    hrow = (r % (n_heads * t_dec)) // t_dec
    trow = r % t_dec
    slope2 = _alibi_slope(hrow, n_heads) * LOG2E
    col = lax.broadcasted_iota(jnp.int32, (1, ncol), 1)
    tok = col // n_heads
    same_head = (col % n_heads) == hrow
    base = jnp.where(same_head, slope2 * tok.astype(F32), NEG_BIG)

    for p in range(pages):
        off = slope2 * ((j * pages + p) * page - past).astype(F32)
        s = _dot_nt(q, k_refs[p][...].astype(BF16)) + (base + off)
        _softmax_step(s, _with_ones(v_refs[p][...].astype(BF16)), m_ref, acc_ref)

    @pl.when(j == pl.num_programs(1) - 1)
    def _():
        nnew = kn_ref.shape[0]
        ok = (tok[:, :nnew] <= trow) & (col[:, :nnew] < n_heads * t_dec)
        s = jnp.where(ok, _dot_nt(q, kn_ref[...]) + base[:, :nnew], NEG_BIG)
        _softmax_step(s, _with_ones(vn_ref[...]), m_ref, acc_ref)
        lam = _lambda_from(lam_ref, lam_init)
        acc = acc_ref[...]
        o = acc[:, :DK_A] / acc[:, DK_A:DK_A + 1]
        half = nrow // 2
        o = o[0:half] - lam * o[half:nrow]
        o = o * lax.rsqrt(jnp.mean(o * o, axis=-1, keepdims=True) + EPS)
        o_ref[...] = o * sg_ref[...] * (1.0 - lam_init)


def _attn_sample(pt_flat, lamv, q_rows, cache_kr, cache_vr, kn, vn, sg, n_pages, page, t_dec, past,
                 lam_init):
    bs, nrow, _ = q_rows.shape
    n_heads = nrow // (2 * t_dec)
    prow = page * n_heads
    pages = _largest_tile(n_pages, 16, 1)
    steps = n_pages // pages

    def page_spec(p):
        return pl.BlockSpec((prow, DK_A), lambda b, j, pt: (pt[b * n_pages + j * pages + p], 0))

    per_b = lambda a: pl.BlockSpec((None,) + a.shape[1:], lambda b, j, pt: (b, 0, 0))
    const = lambda a: pl.BlockSpec(a.shape, lambda b, j, pt: (0, 0))
    grid_spec = pltpu.PrefetchScalarGridSpec(
        num_scalar_prefetch=1,
        grid=(bs, steps),
        in_specs=[const(lamv), per_b(q_rows)] + [page_spec(p) for p in range(pages)]
                 + [page_spec(p) for p in range(pages)] + [per_b(kn), per_b(vn), const(sg)],
        out_specs=pl.BlockSpec((None, nrow // 2, DK_A), lambda b, j, pt: (b, 0, 0)),
        scratch_shapes=[pltpu.VMEM((nrow, 1), F32), pltpu.VMEM((nrow, 2 * DK_A), F32)],
    )
    return pl.pallas_call(
        functools.partial(_attn_sample_kernel, pages=pages, page=page, past=past, t_dec=t_dec,
                          n_heads=n_heads, lam_init=lam_init),
        out_shape=jax.ShapeDtypeStruct((bs, nrow // 2, DK_A), F32),
        grid_spec=grid_spec,
        compiler_params=_cparams(("arbitrary", "arbitrary")),
        name="attn_sample",
    )(pt_flat, lamv, q_rows, *([cache_kr] * pages), *([cache_vr] * pages), kn, vn, sg)


CONV_B_HALO = 32


def _convb_kernel(u_ref, buf_ref, w_ref, b_ref, lg_ref, lb_ref, cv_ref, nb_ref, xp_ref, xs_ref, *, tm):
    t = pl.program_id(1)
    pad = CONV_B_HALO - (CONV_B_WIDTH - 1)
    shifted_rows = tm + CONV_B_HALO - SUBLANES

    @pl.when(t == 0)
    def _():
        xp_ref[0:CONV_B_HALO] = buf_ref[...]

    @pl.when(t > 0)
    def _():
        xp_ref[0:CONV_B_HALO] = xp_ref[tm:tm + CONV_B_HALO]

    xp_ref[CONV_B_HALO:CONV_B_HALO + tm] = u_ref[...]
    for s in range(1, SUBLANES):
        xs_ref[s - 1, 0:shifted_rows] = xp_ref[s:s + shifted_rows]
    ch = min(64, tm)
    for c in range(tm // ch):
        acc = jnp.zeros((ch, u_ref.shape[1]), F32)
        for j in range(CONV_B_WIDTH):
            s = (pad + j) % SUBLANES
            s0 = c * ch + (pad + j) - s
            src = xp_ref if s == 0 else xs_ref.at[s - 1]
            acc = acc + w_ref[j:j + 1, :] * src[s0:s0 + ch, :]
        y = acc + b_ref[...]
        mu = jnp.mean(y, axis=-1, keepdims=True)
        yc = y - mu
        z = yc * lax.rsqrt(jnp.mean(yc * yc, axis=-1, keepdims=True) + EPS) * lg_ref[...] + lb_ref[...]
        cv_ref[c * ch:(c + 1) * ch] = (z * jax.nn.sigmoid(z)).astype(cv_ref.dtype)
    nb_ref[...] = xp_ref[tm:tm + CONV_B_HALO]


def _convb(u3, buf_pad, w_pad, b, lg, lb, out_dtype):
    nb, t, c = u3.shape
    tm = t if t < SUBLANES else _largest_tile(t, 256)
    nt = t // tm
    const = lambda a: pl.BlockSpec(a.shape, lambda bb, tt: (0, 0))
    return pl.pallas_call(
        functools.partial(_convb_kernel, tm=tm),
        out_shape=[jax.ShapeDtypeStruct((nb, t, c), out_dtype),
                   jax.ShapeDtypeStruct((nb, CONV_B_HALO, c), F32)],
        grid=(nb, nt),
        in_specs=[pl.BlockSpec((None, tm, c), lambda bb, tt: (bb, tt, 0)),
                  pl.BlockSpec((None, CONV_B_HALO, c), lambda bb, tt: (bb, 0, 0)),
                  const(w_pad), const(b), const(lg), const(lb)],
        out_specs=[pl.BlockSpec((None, tm, c), lambda bb, tt: (bb, tt, 0)),
                   pl.BlockSpec((None, CONV_B_HALO, c), lambda bb, tt: (bb, 0, 0))],
        scratch_shapes=[pltpu.VMEM((tm + CONV_B_HALO, c), F32),
                        pltpu.VMEM((SUBLANES - 1, -(-tm // SUBLANES) * SUBLANES + CONV_B_HALO - SUBLANES, c), F32)],
        compiler_params=_cparams(("arbitrary", "arbitrary")),
        name="conv_b",
    )(u3, buf_pad, w_pad, b, lg, lb)


def _outproj_even_kernel(o_ref, cv_ref, w_ref, x_ref, g1_ref, gf_ref, sc2_ref, sh2_ref,
                         x1_ref, h2_ref, *, aw):
    mix = (_dot(o_ref[...].astype(BF16), w_ref[0:aw, :])
           + _dot(cv_ref[...].astype(BF16), w_ref[aw:, :]))
    x1 = x_ref[...] + g1_ref[...] * mix
    x1_ref[...] = x1
    h2_ref[...] = _modnorm(x1, gf_ref[...], sc2_ref[...], sh2_ref[...])


def _outproj_even(o, cv, w_bf, x, tm, g1, gf, sc2, sh2, mod_map):
    rows, aw = o.shape
    d = x.shape[1]
    mr = g1.shape[1]
    const = lambda a: pl.BlockSpec(a.shape, lambda i: (0,) * a.ndim)
    mod_spec = pl.BlockSpec((None, mr, d), lambda i: (mod_map(i), 0, 0))
    xrow = pl.BlockSpec((tm, d), lambda i: (i, 0))
    return pl.pallas_call(
        functools.partial(_outproj_even_kernel, aw=aw),
        out_shape=[jax.ShapeDtypeStruct((rows, d), F32)] * 2,
        grid=(rows // tm,),
        in_specs=[pl.BlockSpec((tm, aw), lambda i: (i, 0)),
                  pl.BlockSpec((tm, cv.shape[1]), lambda i: (i, 0)),
                  const(w_bf), xrow, mod_spec, const(gf), mod_spec, mod_spec],
        out_specs=[xrow, xrow],
        compiler_params=_cparams(("arbitrary",)),
        name="outproj_even",
    )(o, cv, w_bf, x, g1, gf, sc2, sh2)


CONV_C_HALO = 8


def _odd_kernel(x_ref, gm_ref, sc1_ref, sh1_ref, win_ref, cw_ref, halo_ref, wout_ref, g1_ref,
                gf_ref, sc2_ref, sh2_ref, x1_ref, h2_ref, nb_ref, zp_ref,
                *, tm, cwid, seq_tiled, t_dec):
    x = x_ref[...]
    h = _modnorm(x, gm_ref[...], sc1_ref[...], sh1_ref[...]).astype(BF16)
    cg = _dot(h, win_ref[:, cwid:2 * cwid])
    hx = _dot(h, win_ref[:, 2 * cwid:3 * cwid])
    z = cg * hx
    halo = CONV_C_HALO
    if seq_tiled:
        t = pl.program_id(1)

        @pl.when(t == 0)
        def _():
            zp_ref[0:halo] = halo_ref[...]

        @pl.when(t > 0)
        def _():
            zp_ref[0:halo] = zp_ref[tm:tm + halo]
    else:
        zp_ref[0:halo] = jnp.zeros((halo, cwid), F32)
    zp_ref[halo:halo + tm] = z
    prev1 = zp_ref[halo - 1:halo - 1 + tm]
    prev2 = zp_ref[halo - 2:halo - 2 + tm]
    if seq_tiled:
        nb_ref[...] = zp_ref[tm:tm + halo]
    else:
        tpos = lax.broadcasted_iota(jnp.int32, (tm, 1), 0) % t_dec
        prev1 = jnp.where(tpos >= 1, prev1, halo_ref[0])
        prev2 = jnp.where(tpos >= 2, prev2, halo_ref[1])
        nb_ref[...] = z
    y = cw_ref[0:1, :] * prev2 + cw_ref[1:2, :] * prev1 + cw_ref[2:3, :] * z
    bg = _dot(h, win_ref[:, 0:cwid])
    mix = _dot((bg * y).astype(BF16), wout_ref[...])
    x1 = x + g1_ref[...] * mix
    x1_ref[...] = x1
    h2_ref[...] = _modnorm(x1, gf_ref[...], sc2_ref[...], sh2_ref[...])


def _odd_mixer(x, nb, t, tm, gm, sc1, sh1, mod_map, win_bf, cw, halo, wout_bf, g1,
               gf, sc2, sh2, seq_tiled, t_dec):
    rows, d = x.shape
    cwid = wout_bf.shape[0]
    mr = g1.shape[1]
    if seq_tiled:
        nt = t // tm
        grid = (nb, nt)
        xmap = lambda b, i: (b * nt + i, 0)
        halo_spec = pl.BlockSpec((None, CONV_C_HALO, cwid), lambda b, i: (b, 0, 0))
        nb_shape = jax.ShapeDtypeStruct((nb, CONV_C_HALO, cwid), F32)
        nb_spec = pl.BlockSpec((None, CONV_C_HALO, cwid), lambda b, i: (b, 0, 0))
        mmap = lambda b, i: (mod_map(b, i), 0, 0)
    else:
        grid = (1, 1)
        xmap = lambda b, i: (0, 0)
        halo_spec = pl.BlockSpec(halo.shape, lambda b, i: (0, 0, 0))
        nb_shape = jax.ShapeDtypeStruct((tm, cwid), F32)
        nb_spec = pl.BlockSpec((tm, cwid), lambda b, i: (0, 0))
        mmap = lambda b, i: (0, 0, 0)
    const = lambda a: pl.BlockSpec(a.shape, lambda b, i: (0,) * a.ndim)
    mod_spec = pl.BlockSpec((None, mr, d), mmap)
    xrow = pl.BlockSpec((tm, d), xmap)
    return pl.pallas_call(
        functools.partial(_odd_kernel, tm=tm, cwid=cwid, seq_tiled=seq_tiled, t_dec=t_dec),
        out_shape=[jax.ShapeDtypeStruct((rows, d), F32)] * 2 + [nb_shape],
        grid=grid,
        in_specs=[xrow, const(gm), mod_spec, mod_spec, const(win_bf), const(cw), halo_spec,
                  const(wout_bf), mod_spec, const(gf), mod_spec, mod_spec],
        out_specs=[xrow, xrow, nb_spec],
        scratch_shapes=[pltpu.VMEM((tm + CONV_C_HALO, cwid), F32)],
        compiler_params=_cparams(("arbitrary", "arbitrary")),
        name="odd_mixer",
    )(x, gm, sc1, sh1, win_bf, cw, halo, wout_bf, g1, gf, sc2, sh2)


def _pack_bf16_pairs(a):
    half = a.shape[1] // 2
    lo = lax.bitcast_convert_type(a[:, :half].astype(F32), jnp.uint32) >> 16
    hi = lax.bitcast_convert_type(a[:, half:].astype(F32), jnp.uint32) & jnp.uint32(0xFFFF0000)
    return lo | hi


def _unpack_bf16_pairs(w):
    lo = lax.bitcast_convert_type(w << 16, F32).astype(BF16)
    hi = lax.bitcast_convert_type(w & jnp.uint32(0xFFFF0000), F32).astype(BF16)
    return lo, hi


def _route_kernel(h_ref, w01_ref, w0_ref, b_ref, tri_ref, cnt0_ref, meta_ref, cnt_ref, hp_ref,
                  carry_ref):
    i = pl.program_id(0)

    @pl.when(i == 0)
    def _():
        carry_ref[...] = cnt0_ref[...]

    h = h_ref[...]
    a0 = h.astype(BF16)
    a1 = (h - a0.astype(F32)).astype(BF16)
    t01 = _dot(a0, w01_ref[...])
    logits = t01[:, :LANES] + t01[:, LANES:] + _dot(a1, w0_ref[...]) + b_ref[...]
    hp_ref[...] = _pack_bf16_pairs(a0)
    tm = logits.shape[0]
    lane = lax.broadcasted_iota(jnp.int32, logits.shape, 1)
    lane_f = lane.astype(F32)
    neg = jnp.float32(-jnp.inf)

    def first_max(vals):
        vmax = jnp.max(vals, axis=-1, keepdims=True)
        idx = jnp.min(jnp.where(vals == vmax, lane_f, float(LANES)), axis=-1, keepdims=True)
        return vmax, idx.astype(jnp.int32)

    gmask = lane < N_GROUPS
    gmax, g_idx = first_max(jnp.where(gmask, logits, neg))
    g_w = 1.0 / jnp.sum(jnp.where(gmask, jnp.exp(logits - gmax), 0.0), axis=-1, keepdims=True)
    lo = N_GROUPS + EXPERTS_PER_GROUP * g_idx
    el = jnp.where((lane >= lo) & (lane < lo + EXPERTS_PER_GROUP), logits, neg)
    v1, i1 = first_max(el)
    v2, i2 = first_max(jnp.where(lane == i1, neg, el))
    tt = jnp.exp(v2 - v1)
    wgt1 = g_w / (1.0 + tt)
    wgt2 = g_w * tt / (1.0 + tt)
    e1 = i1 - N_GROUPS
    e2 = i2 - N_GROUPS
    hit1 = lane == e1
    hit2 = lane == e2
    onehot = (hit1 | hit2).astype(F32)
    cnt = _dot(tri_ref[...], onehot.astype(BF16)) + carry_ref[...]
    r1 = jnp.sum(jnp.where(hit1, cnt, 0.0), axis=-1, keepdims=True)
    r2 = jnp.sum(jnp.where(hit2, cnt, 0.0), axis=-1, keepdims=True)
    carry_ref[...] = carry_ref[...] + jnp.sum(onehot, axis=0, keepdims=True)
    cnt_ref[...] = carry_ref[...]
    cols = (e1.astype(F32), e2.astype(F32), r1, r2, wgt1, wgt2)
    meta = jnp.zeros((tm, LANES), F32)
    for c, val in enumerate(cols):
        meta = jnp.where(lane == c, val, meta)
    meta_ref[...] = meta


def _route(h2, w01, w0, b_pad, cnt0, tm):
    n, d = h2.shape
    tri = (jnp.arange(tm)[:, None] > jnp.arange(tm)[None, :]).astype(BF16)
    const = lambda a: pl.BlockSpec(a.shape, lambda i: (0, 0))
    return pl.pallas_call(
        _route_kernel,
        out_shape=[jax.ShapeDtypeStruct((n, LANES), F32), jax.ShapeDtypeStruct((1, LANES), F32),
                   jax.ShapeDtypeStruct((n, d // 2), jnp.uint32)],
        grid=(n // tm,),
        in_specs=[pl.BlockSpec((tm, d), lambda i: (i, 0)), const(w01), const(w0), const(b_pad),
                  const(tri), const(cnt0)],
        out_specs=[pl.BlockSpec((tm, LANES), lambda i: (i, 0)), const(cnt0),
                   pl.BlockSpec((tm, d // 2), lambda i: (i, 0))],
        scratch_shapes=[pltpu.VMEM((1, LANES), F32)],
        compiler_params=_cparams(("arbitrary",)),
        name="moe_route",
    )(h2, w01, w0, b_pad, tri, cnt0)


def _for_each_row_copy(tm, dest_ref, make_copy, action):
    def body(g, c):
        for u in range(SUBLANES):
            for k in range(TOP_K):
                slot = dest_ref[0, TOP_K * (g * SUBLANES + u) + k]
                action(make_copy(g, u, k, slot))
        return c

    lax.fori_loop(0, tm // SUBLANES, body, 0)


def _dispatch_kernel(dest_ref, h_ref, xp_in_ref, xp_ref, sem, *, tm):
    del xp_in_ref

    def row_copy(g, u, k, slot):
        return pltpu.make_async_copy(h_ref.at[g, pl.ds(u, 1)], xp_ref.at[pl.ds(slot, 1)], sem)

    _for_each_row_copy(tm, dest_ref, row_copy, lambda cp: cp.start())
    _for_each_row_copy(tm, dest_ref, row_copy, lambda cp: cp.wait())


def _dispatch(dest3, h3, xp_zero, tm):
    ng, _, d = h3.shape
    gt = tm // SUBLANES
    return pl.pallas_call(
        functools.partial(_dispatch_kernel, tm=tm),
        out_shape=jax.ShapeDtypeStruct(xp_zero.shape, xp_zero.dtype),
        grid=(ng // gt,),
        in_specs=[pl.BlockSpec((None, 1, TOP_K * tm), lambda i: (i, 0, 0), memory_space=pltpu.SMEM),
                  pl.BlockSpec((gt, SUBLANES, d), lambda i: (i, 0, 0)),
                  pl.BlockSpec(memory_space=pl.ANY)],
        out_specs=pl.BlockSpec(memory_space=pl.ANY),
        input_output_aliases={2: 0},
        scratch_shapes=[pltpu.SemaphoreType.DMA],
        compiler_params=_cparams(("arbitrary",)),
        name="moe_dispatch",
    )(dest3, h3, xp_zero)


def _expert_kernel(be_ref, nu_ref, x_ref, wgu_ref, wd_ref, y_ref, wgu_bf, wd_bf, *, f, n_blocks):
    s = pl.program_id(0)
    b = s - 1

    @pl.when((b >= 0) & (b < nu_ref[0]))
    def _():
        x_lo, x_hi = _unpack_bf16_pairs(x_ref[...])
        half = x_lo.shape[1]
        gate = _dot(x_lo, wgu_bf[0:half, 0:f]) + _dot(x_hi, wgu_bf[half:, 0:f])
        up = _dot(x_lo, wgu_bf[0:half, f:2 * f]) + _dot(x_hi, wgu_bf[half:, f:2 * f])
        act = (gate * jax.nn.sigmoid(gate) * up).astype(BF16)
        y_ref[...] = _dot(act, wd_bf[...])

    @pl.when(b >= nu_ref[0])
    def _():
        y_ref[...] = jnp.zeros(y_ref.shape, F32)

    nxt = be_ref[jnp.minimum(s, n_blocks - 1)]

    @pl.when((s == 0) | (nxt != be_ref[jnp.maximum(b, 0)]))
    def _():
        wgu_bf[...] = wgu_ref[...].astype(BF16)
        wd_bf[...] = wd_ref[...].astype(BF16)


def _experts(block_e, n_used, xp, w_gate_up, w_down, layer):
    n_slots, dh = xp.shape
    d = 2 * dh
    f = w_down.shape[2]
    n_blocks = n_slots // MOE_ROWS
    blk = lambda s: jnp.maximum(s - 1, 0)
    nxt = lambda s, be: be[jnp.minimum(s, n_blocks - 1)]
    grid_spec = pltpu.PrefetchScalarGridSpec(
        num_scalar_prefetch=2,
        grid=(n_blocks + 1,),
        in_specs=[pl.BlockSpec((MOE_ROWS, dh), lambda s, be, nu: (blk(s), 0)),
                  pl.BlockSpec((None, None, d, 2 * f), lambda s, be, nu: (layer, nxt(s, be), 0, 0)),
                  pl.BlockSpec((None, None, f, d), lambda s, be, nu: (layer, nxt(s, be), 0, 0))],
        out_specs=pl.BlockSpec((MOE_ROWS, d), lambda s, be, nu: (blk(s), 0)),
        scratch_shapes=[pltpu.VMEM((d, 2 * f), BF16), pltpu.VMEM((f, d), BF16)],
    )
    return pl.pallas_call(
        functools.partial(_expert_kernel, f=f, n_blocks=n_blocks),
        out_shape=jax.ShapeDtypeStruct((n_slots, d), F32),
        grid_spec=grid_spec,
        compiler_params=_cparams(("arbitrary",)),
        name="moe_experts",
    )(block_e, n_used, xp, w_gate_up, w_down)


def _combine_kernel(dest_ref, x_ref, meta_ref, g2_ref, yp_ref, o_ref, ybuf, sem, *, tm):

    def row_copy(g, u, k, slot):
        return pltpu.make_async_copy(yp_ref.at[pl.ds(slot, 1)], ybuf.at[k, g, pl.ds(u, 1)], sem)

    _for_each_row_copy(tm, dest_ref, row_copy, lambda cp: cp.start())
    _for_each_row_copy(tm, dest_ref, row_copy, lambda cp: cp.wait())
    meta = meta_ref[...]
    y = meta[:, :, 4:5] * ybuf[0] + meta[:, :, 5:6] * ybuf[1]
    o_ref[...] = x_ref[...] + g2_ref[...] * y


def _combine(dest3, x1, meta, g2, mod_map, yp, tm):
    rows, d = x1.shape
    gt = tm // SUBLANES
    as_groups = lambda a: a.reshape(a.shape[0] // SUBLANES, SUBLANES, a.shape[1])
    grow = lambda width: pl.BlockSpec((gt, SUBLANES, width), lambda i: (i, 0, 0))
    if g2.shape[1] == 1:
        g2_spec = pl.BlockSpec((None, 1, d), lambda i: (mod_map(i), 0, 0))
    else:
        g2 = g2.reshape(g2.shape[0], g2.shape[1] // SUBLANES, SUBLANES, d)
        g2_spec = pl.BlockSpec((None,) + g2.shape[1:], lambda i: (mod_map(i), 0, 0, 0))
    out = pl.pallas_call(
        functools.partial(_combine_kernel, tm=tm),
        out_shape=jax.ShapeDtypeStruct((rows // SUBLANES, SUBLANES, d), F32),
        grid=(rows // tm,),
        in_specs=[pl.BlockSpec((None, 1, TOP_K * tm), lambda i: (i, 0, 0), memory_space=pltpu.SMEM),
                  grow(d), grow(LANES), g2_spec, pl.BlockSpec(memory_space=pl.ANY)],
        out_specs=grow(d),
        scratch_shapes=[pltpu.VMEM((TOP_K, gt, SUBLANES, d), F32), pltpu.SemaphoreType.DMA],
        compiler_params=_cparams(("arbitrary",)),
        name="moe_combine",
    )(dest3, as_groups(x1), as_groups(meta), g2, yp)
    return out.reshape(rows, d)


def _moe(h2_p, h2_s, x1_p, x1_s, g2_p, g2_s, w01, w0, b_pad, w_gate_up, w_down, layer, tm_p, seq):
    n_p, d = h2_p.shape
    n_s = h2_s.shape[0]
    n = n_p + n_s
    meta_p, cnt_p, hp_p = _route(h2_p, w01, w0, b_pad, jnp.zeros((1, LANES), F32), tm_p)
    meta_s, counts, hp_s = _route(h2_s, w01, w0, b_pad, cnt_p, n_s)
    sizes = counts[0, :N_EXPERTS].astype(jnp.int32)
    padded = ((sizes + MOE_ROWS - 1) // MOE_ROWS) * MOE_ROWS
    pad_end = jnp.cumsum(padded)
    pad_start = pad_end - padded

    def slots(meta):
        e_tok = meta[:, 0:TOP_K].astype(jnp.int32)
        rank = meta[:, TOP_K:2 * TOP_K].astype(jnp.int32)
        start = jnp.sum(jnp.where(e_tok[:, :, None] == jnp.arange(N_EXPERTS)[None, None, :],
                                  pad_start[None, None, :], 0), axis=-1)
        return (start + rank).astype(jnp.int32)

    dest_p = slots(meta_p).reshape(n_p // tm_p, 1, TOP_K * tm_p)
    dest_s = slots(meta_s).reshape(1, 1, TOP_K * n_s)
    n_blocks = (n * TOP_K + MOE_ROWS - 1) // MOE_ROWS + N_EXPERTS
    blk0 = jnp.arange(n_blocks, dtype=jnp.int32) * MOE_ROWS
    block_e = jnp.minimum(jnp.sum(pad_end[None, :] <= blk0[:, None], axis=-1), N_EXPERTS - 1).astype(jnp.int32)
    n_used = (pad_end[-1:] // MOE_ROWS).astype(jnp.int32)

    as_groups = lambda a: a.reshape(a.shape[0] // SUBLANES, SUBLANES, a.shape[1])
    xp = _dispatch(dest_p, as_groups(hp_p), jnp.zeros((n_blocks * MOE_ROWS, d // 2), jnp.uint32), tm_p)
    xp = _dispatch(dest_s, as_groups(hp_s), xp, n_s)
    yp = _experts(block_e, n_used, xp, w_gate_up, w_down, layer)
    per_seq = seq // tm_p
    out_p = _combine(dest_p, x1_p, meta_p, g2_p, lambda i: i // per_seq, yp, tm_p)
    out_s = _combine(dest_s, x1_s, meta_s, g2_s, lambda i: 0, yp, n_s)
    return out_p, out_s


def kernel(x_prompt, x_sample, cache_k, cache_v, state_conv_b, state_conv_c, page_table, c_prompt, c_sample, norm_mix_g, norm_ffn_g, w_ada, b_ada, w_in_a, q_norm_g, k_norm_g, lam_q1, lam_k1, lam_q2, lam_k2, subln_g, convb_w, convb_b, convb_ln_g, convb_ln_b, w_out_a, w_in_c, convc_w, w_out_c, w_group, b_group, w_router, b_router, w_gate_up, w_down):
    bp, seq, d = x_prompt.shape
    bs, t_dec, _ = x_sample.shape
    depth = w_ada.shape[0]
    n_p, n_s = bp * seq, bs * t_dec
    n = n_p + n_s
    n_pool, page = cache_k.shape[1], cache_k.shape[2]
    n_heads = cache_k.shape[3]
    aw = n_heads * DK_A
    bw = state_conv_b.shape[-1]
    cwid = state_conv_c.shape[-1]
    n_pages = page_table.shape[1]
    past = n_pages * page
    tm_p = _largest_tile(seq, 512, LANES)
    assert n_s % SUBLANES == 0

    x_p = x_prompt.reshape(n_p, d)
    x_s = x_sample.reshape(n_s, d)

    c_all = jnp.concatenate([c_prompt, c_sample], axis=0)
    rows = c_all.shape[0]
    rows_pad = -(-rows // SUBLANES) * SUBLANES
    mods = _ada_all(jnp.pad(c_all, ((0, rows_pad - rows), (0, 0))), w_ada, b_ada)
    mods = mods[:, :rows].reshape(depth, rows, 6, d).transpose(0, 2, 1, 3)
    mods_p = mods[:, :, :bp, None, :]
    mods_s = jnp.repeat(mods[:, :, bp:], t_dec, axis=2)[:, :, None]

    def mod_rows(l, j):
        return mods_p[l, j], mods_s[l, j]

    bd = (jnp.arange(aw)[:, None] // HD_A == jnp.arange(aw)[None, :] // HD_A).astype(BF16)
    cache_kr = cache_k.reshape(-1, DK_A)
    cache_vr = cache_v.reshape(-1, DK_A)
    per_seq = seq // tm_p
    pmap1 = lambda i: i // per_seq
    smap1 = lambda i: 0
    row2 = lambda a: a.reshape(1, -1)

    k_p, v_p, k_s, v_s, cb_p, cb_s, cc_p, cc_s = [], [], [], [], [], [], [], []
    for l in range(depth):
        i = l // 2
        sh1_p, sh1_s = mod_rows(l, 0)
        sc1_p, sc1_s = mod_rows(l, 1)
        g1_p, g1_s = mod_rows(l, 2)
        sh2_p, sh2_s = mod_rows(l, 3)
        sc2_p, sc2_s = mod_rows(l, 4)
        g2_p, g2_s = mod_rows(l, 5)
        gm = row2(norm_mix_g[l])
        gf = row2(norm_ffn_g[l])
        if l % 2 == 0:
            lam_init = 0.8 - 0.6 * math.exp(-0.3 * l)
            w_in = w_in_a[i].astype(BF16)
            w_out = w_out_a[i].astype(BF16)
            qg = jnp.tile(q_norm_g[i].reshape(1, DK_A), (1, n_heads))
            kg = jnp.tile(k_norm_g[i].reshape(1, DK_A), (1, n_heads))
            lamv = jnp.stack([lam_q1[i], lam_k1[i], lam_q2[i], lam_k2[i]])
            sg = row2(subln_g[i])
            cw = jnp.pad(convb_w[i].reshape(CONV_B_WIDTH, bw), ((0, 1), (0, 0)))
            cbb, clg, clb = row2(convb_b[i]), row2(convb_ln_g[i]), row2(convb_ln_b[i])
            halo_pad = ((0, 0), (CONV_B_HALO - (CONV_B_WIDTH - 1), 0), (0, 0))

            qb, kf, kbt, vf, vb, u = _inproj_even(x_p, tm_p, gm, sc1_p, sh1_p, pmap1,
                                                  w_in, bd, qg, kg, aw, bw, True)
            o_p = _attn_prompt(lamv, qb, kbt, vb, sg, bp, seq, lam_init)
            cv_p, nb_p = _convb(u.reshape(bp, seq, bw), jnp.zeros((bp, CONV_B_HALO, bw), F32),
                                cw, cbb, clg, clb, BF16)
            x1_p, h2_p = _outproj_even(o_p, cv_p.reshape(n_p, bw), w_out, x_p, tm_p,
                                       g1_p, gf, sc2_p, sh2_p, pmap1)
            k_p.append(kf.reshape(bp, seq, n_heads, DK_A))
            v_p.append(vf.reshape(bp, seq, n_heads, DK_A))
            cb_p.append(nb_p[:, CONV_B_HALO - (CONV_B_WIDTH - 1):])

            qb, kf, kb, vf, vb, u = _inproj_even(x_s, n_s, gm, sc1_s, sh1_s, smap1,
                                                 w_in, bd, qg, kg, aw, bw, False)
            q4 = qb.reshape(bs, t_dec, n_heads, 2, HD_A).transpose(0, 2, 1, 3, 4)[:, None]
            eye_m = jnp.eye(2, dtype=BF16)[None, :, None, None, :, None]
            q_rows = (q4 * eye_m).reshape(bs, 2 * n_heads * t_dec, DK_A)
            new_pad = ((0, 0), (0, LANES - t_dec * n_heads), (0, 0))
            kn = jnp.pad(kb.reshape(bs, t_dec * n_heads, DK_A), new_pad)
            vn = jnp.pad(vb.reshape(bs, t_dec * n_heads, DK_A), new_pad)
            pt_flat = (page_table.astype(jnp.int32) + i * n_pool).reshape(-1)
            o_s = _attn_sample(pt_flat, lamv, q_rows, cache_kr, cache_vr, kn, vn, sg, n_pages, page,
                               t_dec, past, lam_init)
            o_s = o_s.reshape(bs, n_heads, t_dec, DK_A).transpose(0, 2, 1, 3)
            cv_s, nb_s = _convb(u.reshape(bs, t_dec, bw), jnp.pad(state_conv_b[i], halo_pad),
                                cw, cbb, clg, clb, F32)
            x1_s, h2_s = _outproj_even(o_s.reshape(n_s, aw), cv_s.reshape(n_s, bw), w_out,
                                       x_s, n_s, g1_s, gf, sc2_s, sh2_s, smap1)
            k_s.append(kf.reshape(bs, t_dec, n_heads, DK_A))
            v_s.append(vf.reshape(bs, t_dec, n_heads, DK_A))
            cb_s.append(nb_s[:, CONV_B_HALO - (CONV_B_WIDTH - 1):])
        else:
            w_in = w_in_c[i].astype(BF16)
            w_out = w_out_c[i].astype(BF16)
            cw = jnp.pad(convc_w[i].reshape(CONV_C_WIDTH, cwid), ((0, SUBLANES - CONV_C_WIDTH), (0, 0)))
            halo_p = jnp.zeros((bp, CONV_C_HALO, cwid), F32)
            x1_p, h2_p, nb_p = _odd_mixer(
                x_p, bp, seq, tm_p, gm, sc1_p, sh1_p, lambda b, t: b, w_in, cw, halo_p, w_out,
                g1_p, gf, sc2_p, sh2_p, True, t_dec)
            cc_p.append(nb_p[:, CONV_C_HALO - (CONV_C_WIDTH - 1):])
            st = state_conv_c[i]
            zrow = jnp.zeros((bs, 1, cwid), F32)
            halo1 = jnp.concatenate([st[:, 1:2]] + [zrow] * (t_dec - 1), axis=1).reshape(n_s, cwid)
            halo2 = jnp.concatenate([st[:, 0:1], st[:, 1:2]] + [zrow] * (t_dec - 2), axis=1).reshape(n_s, cwid)
            x1_s, h2_s, z_s = _odd_mixer(
                x_s, 1, n_s, n_s, gm, sc1_s, sh1_s, None, w_in, cw, jnp.stack([halo1, halo2]),
                w_out, g1_s, gf, sc2_s, sh2_s, False, t_dec)
            cc_s.append(z_s.reshape(bs, t_dec, cwid)[:, t_dec - (CONV_C_WIDTH - 1):])

        w_r = jnp.pad(jnp.concatenate([w_group[l], w_router[l]], axis=1),
                      ((0, 0), (0, LANES - N_GROUPS - N_EXPERTS)))
        w0 = w_r.astype(BF16)
        w1 = (w_r - w0.astype(F32)).astype(BF16)
        w01 = jnp.concatenate([w0, w1], axis=1)
        b_r = jnp.pad(jnp.concatenate([b_group[l], b_router[l]]), (0, LANES - N_GROUPS - N_EXPERTS))
        x_p, x_s = _moe(h2_p, h2_s, x1_p, x1_s, g2_p, g2_s, w01, w0, row2(b_r),
                        w_gate_up, w_down, l, tm_p, seq)

    return (x_p.reshape(bp, seq, d), x_s.reshape(bs, t_dec, d),
            jnp.stack(k_p), jnp.stack(v_p), jnp.stack(k_s), jnp.stack(v_s),
            jnp.stack(cb_p), jnp.stack(cb_s), jnp.stack(cc_p), jnp.stack(cc_s))
```

```python
import functools
import math

import jax
import jax.numpy as jnp
from jax import lax
from jax.experimental import pallas as pl
from jax.experimental.pallas import tpu as pltpu

F32 = jnp.float32
BF16 = jnp.bfloat16

HD_A = 64
DK_A = 2 * HD_A
N_GROUPS = 4
EXPERTS_PER_GROUP = 8
N_EXPERTS = N_GROUPS * EXPERTS_PER_GROUP
TOP_K = 2
CONV_B_WIDTH = 31
CONV_C_WIDTH = 3
ALIBI_MAX = 8.0
EPS = 1e-6
NEG_BIG = -1e30
LOG2E = math.log2(math.e)

LANES = 128
SUBLANES = 8
MOE_ROWS = 256
STAB_LIMIT = 60.0
BOUND_SLACK = 1.01
ATTN_TILE = 512
ADA_COLS = 1536
VMEM_LIMIT = 56 * 1024 * 1024


def _cparams(sem, vmem=VMEM_LIMIT):
    return pltpu.CompilerParams(dimension_semantics=sem, vmem_limit_bytes=vmem)


def _largest_tile(n, cap, mult=SUBLANES):
    best = None
    for t in range(mult, min(n, cap) + 1, mult):
        if n % t == 0:
            best = t
    assert best is not None, (n, cap, mult)
    return best


def _modnorm(x, g, scale, shift):
    ms = jnp.mean(x * x, axis=-1, keepdims=True)
    return (x * lax.rsqrt(ms + EPS)) * g * (1.0 + scale) + shift


def _dot(a, b):
    return jnp.dot(a, b, preferred_element_type=F32)


def _dot_nt(a, b):
    return lax.dot_general(a, b, (((1,), (1,)), ((), ())), preferred_element_type=F32)


def _split3(a):
    a0 = a.astype(BF16)
    r1 = a - a0.astype(F32)
    a1 = r1.astype(BF16)
    a2 = (r1 - a1.astype(F32)).astype(BF16)
    return a0, a1, a2


def _ada_kernel(c_ref, w_ref, b_ref, o_ref):
    c = c_ref[...]
    a = (c * jax.nn.sigmoid(c)).astype(BF16)
    o_ref[...] = _dot(a, w_ref[...].astype(BF16)) + b_ref[...]


def _ada_all(c_pad, w_ada, b_ada):
    depth, d, n6 = w_ada.shape
    rows = c_pad.shape[0]
    tn = _largest_tile(n6, ADA_COLS, LANES)
    return pl.pallas_call(
        _ada_kernel,
        out_shape=jax.ShapeDtypeStruct((depth, rows, n6), F32),
        grid=(depth, n6 // tn),
        in_specs=[pl.BlockSpec((rows, d), lambda l, j: (0, 0)),
                  pl.BlockSpec((None, d, tn), lambda l, j: (l, 0, j)),
                  pl.BlockSpec((None, 1, tn), lambda l, j: (l, 0, j))],
        out_specs=pl.BlockSpec((None, rows, tn), lambda l, j: (l, 0, j)),
        compiler_params=_cparams(("arbitrary", "arbitrary")),
        name="ada_mod",
    )(c_pad, w_ada, b_ada.reshape(depth, 1, n6))


def _inproj_even_kernel(x_ref, g_ref, sc_ref, sh_ref, w_ref, bd_ref, qg_ref, kg_ref,
                        qb_ref, kf_ref, kb_ref, vf_ref, vb_ref, u_ref, *, aw, bw, k_transposed):
    h = _modnorm(x_ref[...], g_ref[...], sc_ref[...], sh_ref[...]).astype(BF16)
    bd = bd_ref[...]

    def group_norm(t, gain):
        tt = t * t
        hi = tt.astype(BF16)
        lo = (tt - hi.astype(F32)).astype(BF16)
        ss = _dot(hi, bd) + _dot(lo, bd)
        return t * lax.rsqrt(ss * (1.0 / HD_A) + EPS) * gain

    n_heads = aw // DK_A
    tm = x_ref.shape[0]

    def store_head_rows(ref, t):
        for hh in range(n_heads):
            ref[pl.ds(hh, tm, stride=n_heads), :] = t[:, hh * DK_A:(hh + 1) * DK_A]

    q = group_norm(_dot(h, w_ref[:, 0:aw]), qg_ref[...])
    qb_ref[...] = (q * (HD_A ** -0.5 * LOG2E)).astype(BF16)
    k = group_norm(_dot(h, w_ref[:, aw:2 * aw]), kg_ref[...])
    store_head_rows(kf_ref, k)
    kb_ref[...] = (k.T if k_transposed else k).astype(BF16)
    v = _dot(h, w_ref[:, 2 * aw:3 * aw])
    store_head_rows(vf_ref, v)
    vb_ref[...] = v.astype(BF16)
    ga = _dot(h, w_ref[:, 3 * aw:3 * aw + bw])
    gb = _dot(h, w_ref[:, 3 * aw + bw:3 * aw + 2 * bw])
    u_ref[...] = ga * jax.nn.sigmoid(gb)


def _inproj_even(x, tm, g, sc, sh, mod_map, w_bf, bd, qg, kg, aw, bw, k_transposed):
    rows, d = x.shape
    nt = rows // tm
    mr = sc.shape[1]
    n_heads = aw // DK_A
    row_spec = lambda width: pl.BlockSpec((tm, width), lambda i: (i, 0))
    head_rows = pl.BlockSpec((tm * n_heads, DK_A), lambda i: (i, 0))
    const = lambda a: pl.BlockSpec(a.shape, lambda i: (0,) * a.ndim)
    mod_spec = pl.BlockSpec((None, mr, d), lambda i: (mod_map(i), 0, 0))
    kb_shape = (aw, rows) if k_transposed else (rows, aw)
    kb_spec = pl.BlockSpec((aw, tm), lambda i: (0, i)) if k_transposed else row_spec(aw)
    outs = [jax.ShapeDtypeStruct((rows, aw), BF16), jax.ShapeDtypeStruct((rows * n_heads, DK_A), F32),
            jax.ShapeDtypeStruct(kb_shape, BF16), jax.ShapeDtypeStruct((rows * n_heads, DK_A), F32),
            jax.ShapeDtypeStruct((rows, aw), BF16), jax.ShapeDtypeStruct((rows, bw), F32)]
    return pl.pallas_call(
        functools.partial(_inproj_even_kernel, aw=aw, bw=bw, k_transposed=k_transposed),
        out_shape=outs,
        grid=(nt,),
        in_specs=[pl.BlockSpec((tm, d), lambda i: (i, 0)), const(g), mod_spec, mod_spec,
                  const(w_bf), const(bd), const(qg), const(kg)],
        out_specs=[row_spec(aw), head_rows, kb_spec, head_rows, row_spec(aw), row_spec(bw)],
        compiler_params=_cparams(("arbitrary",)),
        name="inproj_even",
    )(x, g, sc, sh, w_bf, bd, qg, kg)


def _lambda_from(lam_ref, lam_init):
    lv = lam_ref[...]
    a = jnp.sum(lv[0:1] * lv[1:2], axis=-1, keepdims=True)
    b = jnp.sum(lv[2:3] * lv[3:4], axis=-1, keepdims=True)
    return jnp.exp(a) - jnp.exp(b) + lam_init


def _alibi_slope(head, n_heads):
    slope = jnp.float32(0.0)
    for hh in range(n_heads):
        slope = jnp.where(head == hh, jnp.float32(2.0 ** (-ALIBI_MAX * (hh + 1) / n_heads)), slope)
    return slope


def _softmax_step(s, v_ones, m_ref, acc_ref):
    m_old = m_ref[...]
    m_new = jnp.maximum(m_old, jnp.max(s, axis=-1, keepdims=True))
    p = jnp.exp2(s - m_new)
    acc_ref[...] = jnp.exp2(m_old - m_new) * acc_ref[...] + _dot(p.astype(BF16), v_ones)
    m_ref[...] = m_new


def _with_ones(v):
    return jnp.concatenate([v, jnp.ones(v.shape, v.dtype)], axis=-1)


def _attn_prompt_kernel(lam_ref, q_ref, kt_ref, v_ref, sg_ref, o_ref, kmax_ref, m1_ref, m2_ref,
                        acc1_ref, acc2_ref, *, tq, lam_init, n_heads):
    h = pl.program_id(1)
    qi = pl.program_id(2)

    @pl.when(qi == 0)
    def _():
        kk = kt_ref[...].astype(F32)
        kk = kk * kk
        for m in range(2):
            n2 = jnp.sum(kk[m * HD_A:(m + 1) * HD_A], axis=0, keepdims=True)
            kmax_ref[m] = jnp.sqrt(jnp.max(n2, axis=-1, keepdims=True))

    for acc_ref in (acc1_ref, acc2_ref):
        acc_ref[...] = jnp.zeros(acc_ref.shape, F32)
    slope2 = _alibi_slope(h, n_heads) * LOG2E
    q = q_ref[...]
    lane = lax.broadcasted_iota(jnp.int32, q.shape, 1)
    zero = jnp.zeros_like(q)
    q1 = jnp.where(lane < HD_A, q, zero)
    q2 = jnp.where(lane >= HD_A, q, zero)
    col = lax.broadcasted_iota(jnp.int32, (1, tq), 1)
    bound = []
    for m, qm in enumerate((q1, q2)):
        qf = qm.astype(F32)
        qn = jnp.sqrt(jnp.sum(qf * qf, axis=-1, keepdims=True))
        bound.append(qn * kmax_ref[m] * BOUND_SLACK)
    bounded = jnp.max(jnp.maximum(bound[0], bound[1])) < STAB_LIMIT

    def tiles(kj):
        r0 = pl.multiple_of(kj * tq, tq)
        colb = slope2 * ((kj - qi) * tq + col).astype(F32)
        return kt_ref[:, pl.ds(r0, tq)], _with_ones(v_ref[pl.ds(r0, tq), :]), colb

    def causal_keep():
        return (lax.broadcasted_iota(jnp.int32, (tq, tq), 0)
                >= lax.broadcasted_iota(jnp.int32, (tq, tq), 1))

    def run(step):
        def body(kj, carry):
            step(kj, False)
            return carry

        lax.fori_loop(0, qi, body, 0)
        step(qi, True)

    @pl.when(bounded)
    def _():
        row = lax.broadcasted_iota(jnp.int32, (tq, 1), 0).astype(F32)
        lane2 = lax.broadcasted_iota(jnp.int32, (tq, LANES), 1)
        sub = lax.broadcasted_iota(jnp.int32, (2 * SUBLANES, tq), 0)
        q_aug = []
        for qm, bm in ((q1, bound[0]), (q2, bound[1])):
            rt = -(bm + slope2 * row)
            rt_hi = rt.astype(BF16).astype(F32)
            rt_lo = (rt - rt_hi).astype(BF16).astype(F32)
            ext = jnp.where(lane2 == 0, rt_hi,
                            jnp.where(lane2 == 1, rt_lo, jnp.where(lane2 < 5, 1.0, 0.0)))
            q_aug.append(jnp.concatenate([qm, ext.astype(BF16)], axis=-1))
        k_pad = jnp.zeros((LANES - 2 * SUBLANES, tq), BF16)

        def step(kj, masked):
            kt, v_ones, colb = tiles(kj)
            c0, c1, c2 = (c.astype(F32) for c in _split3(colb))
            ext_k = jnp.where(sub < 2, 1.0,
                              jnp.where(sub == 2, c0,
                                        jnp.where(sub == 3, c1, jnp.where(sub == 4, c2, 0.0))))
            kt_aug = jnp.concatenate([kt, ext_k.astype(BF16), k_pad], axis=0)
            for qa, acc_ref in ((q_aug[0], acc1_ref), (q_aug[1], acc2_ref)):
                e = _dot(qa, kt_aug)
                if masked:
                    e = jnp.where(causal_keep(), e, NEG_BIG)
                acc_ref[...] += _dot(jnp.exp2(e).astype(BF16), v_ones)

        run(step)

    @pl.when(jnp.logical_not(bounded))
    def _():
        for m_ref in (m1_ref, m2_ref):
            m_ref[...] = jnp.full(m_ref.shape, NEG_BIG, F32)

        def step(kj, masked):
            kt, v_ones, colb = tiles(kj)
            for qm, m_ref, acc_ref in ((q1, m1_ref, acc1_ref), (q2, m2_ref, acc2_ref)):
                s = _dot(qm, kt) + colb
                if masked:
                    s = jnp.where(causal_keep(), s, NEG_BIG)
                _softmax_step(s, v_ones, m_ref, acc_ref)

        run(step)

    lam = _lambda_from(lam_ref, lam_init)
    a1 = acc1_ref[...]
    a2 = acc2_ref[...]
    o = a1[:, :DK_A] / a1[:, DK_A:DK_A + 1] - lam * (a2[:, :DK_A] / a2[:, DK_A:DK_A + 1])
    o = o * lax.rsqrt(jnp.mean(o * o, axis=-1, keepdims=True) + EPS)
    o_ref[...] = (o * sg_ref[...] * (1.0 - lam_init)).astype(o_ref.dtype)


def _attn_prompt(lamv, qb, kbt, vb, sg, batch, seq, lam_init):
    n_heads = qb.shape[1] // DK_A
    tq = _largest_tile(seq, ATTN_TILE, LANES)
    nq = seq // tq
    q_spec = pl.BlockSpec((tq, DK_A), lambda b, h, i: (b * nq + i, h))
    kt_spec = pl.BlockSpec((DK_A, seq), lambda b, h, i: (h, b))
    kv_spec = pl.BlockSpec((seq, DK_A), lambda b, h, i: (b, h))
    return pl.pallas_call(
        functools.partial(_attn_prompt_kernel, tq=tq, lam_init=lam_init, n_heads=n_heads),
        out_shape=jax.ShapeDtypeStruct(qb.shape, BF16),
        grid=(batch, n_heads, nq),
        in_specs=[pl.BlockSpec(lamv.shape, lambda b, h, i: (0, 0)), q_spec, kt_spec, kv_spec,
                  pl.BlockSpec(sg.shape, lambda b, h, i: (0, 0))],
        out_specs=q_spec,
        scratch_shapes=[pltpu.VMEM((2, 1, 1), F32), pltpu.VMEM((tq, 1), F32), pltpu.VMEM((tq, 1), F32),
                        pltpu.VMEM((tq, 2 * DK_A), F32), pltpu.VMEM((tq, 2 * DK_A), F32)],
        compiler_params=_cparams(("arbitrary",) * 3),
        name="attn_prompt",
    )(lamv, qb, kbt, vb, sg)


def _attn_sample_kernel(pt_ref, lam_ref, q_ref, *rest, pages, page, past, t_dec, n_heads, lam_init):
    k_refs = rest[:pages]
    v_refs = rest[pages:2 * pages]
    kn_ref, vn_ref, sg_ref, o_ref, m_ref, acc_ref = rest[2 * pages:]
    j = pl.program_id(1)
    nrow = 2 * n_heads * t_dec
    ncol = page * n_heads

    @pl.when(j == 0)
    def _():
        m_ref[...] = jnp.full(m_ref.shape, NEG_BIG, F32)
        acc_ref[...] = jnp.zeros(acc_ref.shape, F32)

    q = q_ref[...]
    r = lax.broadcasted_iota(jnp.int32, (nrow, 1), 0)
    hrow = (r % (n_heads * t_dec)) // t_dec
    trow = r % t_dec
    slope2 = _alibi_slope(hrow, n_heads) * LOG2E
    col = lax.broadcasted_iota(jnp.int32, (1, ncol), 1)
    tok = col // n_heads
    same_head = (col % n_heads) == hrow
    base = jnp.where(same_head, slope2 * tok.astype(F32), NEG_BIG)

    for p in range(pages):
        off = slope2 * ((j * pages + p) * page - past).astype(F32)
        s = _dot_nt(q, k_refs[p][...].astype(BF16)) + (base + off)
        _softmax_step(s, _with_ones(v_refs[p][...].astype(BF16)), m_ref, acc_ref)

    @pl.when(j == pl.num_programs(1) - 1)
    def _():
        nnew = kn_ref.shape[0]
        ok = (tok[:, :nnew] <= trow) & (col[:, :nnew] < n_heads * t_dec)
        s = jnp.where(ok, _dot_nt(q, kn_ref[...]) + base[:, :nnew], NEG_BIG)
        _softmax_step(s, _with_ones(vn_ref[...]), m_ref, acc_ref)
        lam = _lambda_from(lam_ref, lam_init)
        acc = acc_ref[...]
        o = acc[:, :DK_A] / acc[:, DK_A:DK_A + 1]
        half = nrow // 2
        o = o[0:half] - lam * o[half:nrow]
        o = o * lax.rsqrt(jnp.mean(o * o, axis=-1, keepdims=True) + EPS)
        o_ref[...] = o * sg_ref[...] * (1.0 - lam_init)


def _attn_sample(pt_flat, lamv, q_rows, cache_kr, cache_vr, kn, vn, sg, n_pages, page, t_dec, past,
                 lam_init):
    bs, nrow, _ = q_rows.shape
    n_heads = nrow // (2 * t_dec)
    prow = page * n_heads
    pages = _largest_tile(n_pages, 16, 1)
    steps = n_pages // pages

    def page_spec(p):
        return pl.BlockSpec((prow, DK_A), lambda b, j, pt: (pt[b * n_pages + j * pages + p], 0))

    per_b = lambda a: pl.BlockSpec((None,) + a.shape[1:], lambda b, j, pt: (b, 0, 0))
    const = lambda a: pl.BlockSpec(a.shape, lambda b, j, pt: (0, 0))
    grid_spec = pltpu.PrefetchScalarGridSpec(
        num_scalar_prefetch=1,
        grid=(bs, steps),
        in_specs=[const(lamv), per_b(q_rows)] + [page_spec(p) for p in range(pages)]
                 + [page_spec(p) for p in range(pages)] + [per_b(kn), per_b(vn), const(sg)],
        out_specs=pl.BlockSpec((None, nrow // 2, DK_A), lambda b, j, pt: (b, 0, 0)),
        scratch_shapes=[pltpu.VMEM((nrow, 1), F32), pltpu.VMEM((nrow, 2 * DK_A), F32)],
    )
    return pl.pallas_call(
        functools.partial(_attn_sample_kernel, pages=pages, page=page, past=past, t_dec=t_dec,
                          n_heads=n_heads, lam_init=lam_init),
        out_shape=jax.ShapeDtypeStruct((bs, nrow // 2, DK_A), F32),
        grid_spec=grid_spec,
        compiler_params=_cparams(("arbitrary", "arbitrary")),
        name="attn_sample",
    )(pt_flat, lamv, q_rows, *([cache_kr] * pages), *([cache_vr] * pages), kn, vn, sg)


CONV_B_HALO = 32


def _convb_kernel(u_ref, buf_ref, w_ref, b_ref, lg_ref, lb_ref, cv_ref, nb_ref, xp_ref, xs_ref, *, tm):
    t = pl.program_id(1)
    pad = CONV_B_HALO - (CONV_B_WIDTH - 1)
    shifted_rows = tm + CONV_B_HALO - SUBLANES

    @pl.when(t == 0)
    def _():
        xp_ref[0:CONV_B_HALO] = buf_ref[...]

    @pl.when(t > 0)
    def _():
        xp_ref[0:CONV_B_HALO] = xp_ref[tm:tm + CONV_B_HALO]

    xp_ref[CONV_B_HALO:CONV_B_HALO + tm] = u_ref[...]
    for s in range(1, SUBLANES):
        xs_ref[s - 1, 0:shifted_rows] = xp_ref[s:s + shifted_rows]
    ch = min(64, tm)
    for c in range(tm // ch):
        acc = jnp.zeros((ch, u_ref.shape[1]), F32)
        for j in range(CONV_B_WIDTH):
            s = (pad + j) % SUBLANES
            s0 = c * ch + (pad + j) - s
            src = xp_ref if s == 0 else xs_ref.at[s - 1]
            acc = acc + w_ref[j:j + 1, :] * src[s0:s0 + ch, :]
        y = acc + b_ref[...]
        mu = jnp.mean(y, axis=-1, keepdims=True)
        yc = y - mu
        z = yc * lax.rsqrt(jnp.mean(yc * yc, axis=-1, keepdims=True) + EPS) * lg_ref[...] + lb_ref[...]
        cv_ref[c * ch:(c + 1) * ch] = (z * jax.nn.sigmoid(z)).astype(cv_ref.dtype)
    nb_ref[...] = xp_ref[tm:tm + CONV_B_HALO]


def _convb(u3, buf_pad, w_pad, b, lg, lb, out_dtype):
    nb, t, c = u3.shape
    tm = t if t < SUBLANES else _largest_tile(t, 256)
    nt = t // tm
    const = lambda a: pl.BlockSpec(a.shape, lambda bb, tt: (0, 0))
    return pl.pallas_call(
        functools.partial(_convb_kernel, tm=tm),
        out_shape=[jax.ShapeDtypeStruct((nb, t, c), out_dtype),
                   jax.ShapeDtypeStruct((nb, CONV_B_HALO, c), F32)],
        grid=(nb, nt),
        in_specs=[pl.BlockSpec((None, tm, c), lambda bb, tt: (bb, tt, 0)),
                  pl.BlockSpec((None, CONV_B_HALO, c), lambda bb, tt: (bb, 0, 0)),
                  const(w_pad), const(b), const(lg), const(lb)],
        out_specs=[pl.BlockSpec((None, tm, c), lambda bb, tt: (bb, tt, 0)),
                   pl.BlockSpec((None, CONV_B_HALO, c), lambda bb, tt: (bb, 0, 0))],
        scratch_shapes=[pltpu.VMEM((tm + CONV_B_HALO, c), F32),
                        pltpu.VMEM((SUBLANES - 1, -(-tm // SUBLANES) * SUBLANES + CONV_B_HALO - SUBLANES, c), F32)],
        compiler_params=_cparams(("arbitrary", "arbitrary")),
        name="conv_b",
    )(u3, buf_pad, w_pad, b, lg, lb)


def _outproj_even_kernel(o_ref, cv_ref, w_ref, x_ref, g1_ref, gf_ref, sc2_ref, sh2_ref,
                         x1_ref, h2_ref, *, aw):
    mix = (_dot(o_ref[...].astype(BF16), w_ref[0:aw, :])
           + _dot(cv_ref[...].astype(BF16), w_ref[aw:, :]))
    x1 = x_ref[...] + g1_ref[...] * mix
    x1_ref[...] = x1
    h2_ref[...] = _modnorm(x1, gf_ref[...], sc2_ref[...], sh2_ref[...])


def _outproj_even(o, cv, w_bf, x, tm, g1, gf, sc2, sh2, mod_map):
    rows, aw = o.shape
    d = x.shape[1]
    mr = g1.shape[1]
    const = lambda a: pl.BlockSpec(a.shape, lambda i: (0,) * a.ndim)
    mod_spec = pl.BlockSpec((None, mr, d), lambda i: (mod_map(i), 0, 0))
    xrow = pl.BlockSpec((tm, d), lambda i: (i, 0))
    return pl.pallas_call(
        functools.partial(_outproj_even_kernel, aw=aw),
        out_shape=[jax.ShapeDtypeStruct((rows, d), F32)] * 2,
        grid=(rows // tm,),
        in_specs=[pl.BlockSpec((tm, aw), lambda i: (i, 0)),
                  pl.BlockSpec((tm, cv.shape[1]), lambda i: (i, 0)),
                  const(w_bf), xrow, mod_spec, const(gf), mod_spec, mod_spec],
        out_specs=[xrow, xrow],
        compiler_params=_cparams(("arbitrary",)),
        name="outproj_even",
    )(o, cv, w_bf, x, g1, gf, sc2, sh2)


CONV_C_HALO = 8


def _odd_kernel(x_ref, gm_ref, sc1_ref, sh1_ref, win_ref, cw_ref, halo_ref, wout_ref, g1_ref,
                gf_ref, sc2_ref, sh2_ref, x1_ref, h2_ref, nb_ref, zp_ref,
                *, tm, cwid, seq_tiled, t_dec):
    x = x_ref[...]
    h = _modnorm(x, gm_ref[...], sc1_ref[...], sh1_ref[...]).astype(BF16)
    cg = _dot(h, win_ref[:, cwid:2 * cwid])
    hx = _dot(h, win_ref[:, 2 * cwid:3 * cwid])
    z = cg * hx
    halo = CONV_C_HALO
    if seq_tiled:
        t = pl.program_id(1)

        @pl.when(t == 0)
        def _():
            zp_ref[0:halo] = halo_ref[...]

        @pl.when(t > 0)
        def _():
            zp_ref[0:halo] = zp_ref[tm:tm + halo]
    else:
        zp_ref[0:halo] = jnp.zeros((halo, cwid), F32)
    zp_ref[halo:halo + tm] = z
    prev1 = zp_ref[halo - 1:halo - 1 + tm]
    prev2 = zp_ref[halo - 2:halo - 2 + tm]
    if seq_tiled:
        nb_ref[...] = zp_ref[tm:tm + halo]
    else:
        tpos = lax.broadcasted_iota(jnp.int32, (tm, 1), 0) % t_dec
        prev1 = jnp.where(tpos >= 1, prev1, halo_ref[0])
        prev2 = jnp.where(tpos >= 2, prev2, halo_ref[1])
        nb_ref[...] = z
    y = cw_ref[0:1, :] * prev2 + cw_ref[1:2, :] * prev1 + cw_ref[2:3, :] * z
    bg = _dot(h, win_ref[:, 0:cwid])
    mix = _dot((bg * y).astype(BF16), wout_ref[...])
    x1 = x + g1_ref[...] * mix
    x1_ref[...] = x1
    h2_ref[...] = _modnorm(x1, gf_ref[...], sc2_ref[...], sh2_ref[...])


def _odd_mixer(x, nb, t, tm, gm, sc1, sh1, mod_map, win_bf, cw, halo, wout_bf, g1,
               gf, sc2, sh2, seq_tiled, t_dec):
    rows, d = x.shape
    cwid = wout_bf.shape[0]
    mr = g1.shape[1]
    if seq_tiled:
        nt = t // tm
        grid = (nb, nt)
        xmap = lambda b, i: (b * nt + i, 0)
        halo_spec = pl.BlockSpec((None, CONV_C_HALO, cwid), lambda b, i: (b, 0, 0))
        nb_shape = jax.ShapeDtypeStruct((nb, CONV_C_HALO, cwid), F32)
        nb_spec = pl.BlockSpec((None, CONV_C_HALO, cwid), lambda b, i: (b, 0, 0))
        mmap = lambda b, i: (mod_map(b, i), 0, 0)
    else:
        grid = (1, 1)
        xmap = lambda b, i: (0, 0)
        halo_spec = pl.BlockSpec(halo.shape, lambda b, i: (0, 0, 0))
        nb_shape = jax.ShapeDtypeStruct((tm, cwid), F32)
        nb_spec = pl.BlockSpec((tm, cwid), lambda b, i: (0, 0))
        mmap = lambda b, i: (0, 0, 0)
    const = lambda a: pl.BlockSpec(a.shape, lambda b, i: (0,) * a.ndim)
    mod_spec = pl.BlockSpec((None, mr, d), mmap)
    xrow = pl.BlockSpec((tm, d), xmap)
    return pl.pallas_call(
        functools.partial(_odd_kernel, tm=tm, cwid=cwid, seq_tiled=seq_tiled, t_dec=t_dec),
        out_shape=[jax.ShapeDtypeStruct((rows, d), F32)] * 2 + [nb_shape],
        grid=grid,
        in_specs=[xrow, const(gm), mod_spec, mod_spec, const(win_bf), const(cw), halo_spec,
                  const(wout_bf), mod_spec, const(gf), mod_spec, mod_spec],
        out_specs=[xrow, xrow, nb_spec],
        scratch_shapes=[pltpu.VMEM((tm + CONV_C_HALO, cwid), F32)],
        compiler_params=_cparams(("arbitrary", "arbitrary")),
        name="odd_mixer",
    )(x, gm, sc1, sh1, win_bf, cw, halo, wout_bf, g1, gf, sc2, sh2)


def _pack_bf16_pairs(a):
    half = a.shape[1] // 2
    lo = lax.bitcast_convert_type(a[:, :half].astype(F32), jnp.uint32) >> 16
    hi = lax.bitcast_convert_type(a[:, half:].astype(F32), jnp.uint32) & jnp.uint32(0xFFFF0000)
    return lo | hi


def _unpack_bf16_pairs(w):
    lo = lax.bitcast_convert_type(w << 16, F32).astype(BF16)
    hi = lax.bitcast_convert_type(w & jnp.uint32(0xFFFF0000), F32).astype(BF16)
    return lo, hi


def _route_kernel(h_ref, w_ref, b_ref, tri_ref, cnt0_ref, meta_ref, cnt_ref, hp_ref, carry_ref):
    i = pl.program_id(0)

    @pl.when(i == 0)
    def _():
        carry_ref[...] = cnt0_ref[...]

    a0 = h_ref[...].astype(BF16)
    logits = _dot(a0, w_ref[...]) + b_ref[...]
    hp_ref[...] = _pack_bf16_pairs(a0)
    tm = logits.shape[0]
    lane = lax.broadcasted_iota(jnp.int32, logits.shape, 1)
    lane_f = lane.astype(F32)
    neg = jnp.float32(-jnp.inf)

    def first_max(vals):
        vmax = jnp.max(vals, axis=-1, keepdims=True)
        idx = jnp.min(jnp.where(vals == vmax, lane_f, float(LANES)), axis=-1, keepdims=True)
        return vmax, idx.astype(jnp.int32)

    gmask = lane < N_GROUPS
    gmax, g_idx = first_max(jnp.where(gmask, logits, neg))
    g_w = 1.0 / jnp.sum(jnp.where(gmask, jnp.exp(logits - gmax), 0.0), axis=-1, keepdims=True)
    lo = N_GROUPS + EXPERTS_PER_GROUP * g_idx
    el = jnp.where((lane >= lo) & (lane < lo + EXPERTS_PER_GROUP), logits, neg)
    v1, i1 = first_max(el)
    v2, i2 = first_max(jnp.where(lane == i1, neg, el))
    tt = jnp.exp(v2 - v1)
    wgt1 = g_w / (1.0 + tt)
    wgt2 = g_w * tt / (1.0 + tt)
    e1 = i1 - N_GROUPS
    e2 = i2 - N_GROUPS
    hit1 = lane == e1
    hit2 = lane == e2
    onehot = (hit1 | hit2).astype(F32)
    cnt = _dot(tri_ref[...], onehot.astype(BF16)) + carry_ref[...]
    r1 = jnp.sum(jnp.where(hit1, cnt, 0.0), axis=-1, keepdims=True)
    r2 = jnp.sum(jnp.where(hit2, cnt, 0.0), axis=-1, keepdims=True)
    carry_ref[...] = carry_ref[...] + jnp.sum(onehot, axis=0, keepdims=True)
    cnt_ref[...] = carry_ref[...]
    cols = (e1.astype(F32), e2.astype(F32), r1, r2, wgt1, wgt2)
    meta = jnp.zeros((tm, LANES), F32)
    for c, val in enumerate(cols):
        meta = jnp.where(lane == c, val, meta)
    meta_ref[...] = meta


def _route(h2, w_bf, b_pad, cnt0, tm):
    n, d = h2.shape
    tri = (jnp.arange(tm)[:, None] > jnp.arange(tm)[None, :]).astype(BF16)
    const = lambda a: pl.BlockSpec(a.shape, lambda i: (0, 0))
    return pl.pallas_call(
        _route_kernel,
        out_shape=[jax.ShapeDtypeStruct((n, LANES), F32), jax.ShapeDtypeStruct((1, LANES), F32),
                   jax.ShapeDtypeStruct((n, d // 2), jnp.uint32)],
        grid=(n // tm,),
        in_specs=[pl.BlockSpec((tm, d), lambda i: (i, 0)), const(w_bf), const(b_pad),
                  const(tri), const(cnt0)],
        out_specs=[pl.BlockSpec((tm, LANES), lambda i: (i, 0)), const(cnt0),
                   pl.BlockSpec((tm, d // 2), lambda i: (i, 0))],
        scratch_shapes=[pltpu.VMEM((1, LANES), F32)],
        compiler_params=_cparams(("arbitrary",)),
        name="moe_route",
    )(h2, w_bf, b_pad, tri, cnt0)


def _for_each_row_copy(tm, dest_ref, make_copy, action):
    def body(g, c):
        for u in range(SUBLANES):
            for k in range(TOP_K):
                slot = dest_ref[0, TOP_K * (g * SUBLANES + u) + k]
                action(make_copy(g, u, k, slot))
        return c

    lax.fori_loop(0, tm // SUBLANES, body, 0)


def _dispatch_kernel(dest_ref, h_ref, xp_in_ref, xp_ref, sem, *, tm):
    del xp_in_ref

    def row_copy(g, u, k, slot):
        return pltpu.make_async_copy(h_ref.at[g, pl.ds(u, 1)], xp_ref.at[pl.ds(slot, 1)], sem)

    _for_each_row_copy(tm, dest_ref, row_copy, lambda cp: cp.start())
    _for_each_row_copy(tm, dest_ref, row_copy, lambda cp: cp.wait())


def _dispatch(dest3, h3, xp_zero, tm):
    ng, _, d = h3.shape
    gt = tm // SUBLANES
    return pl.pallas_call(
        functools.partial(_dispatch_kernel, tm=tm),
        out_shape=jax.ShapeDtypeStruct(xp_zero.shape, xp_zero.dtype),
        grid=(ng // gt,),
        in_specs=[pl.BlockSpec((None, 1, TOP_K * tm), lambda i: (i, 0, 0), memory_space=pltpu.SMEM),
                  pl.BlockSpec((gt, SUBLANES, d), lambda i: (i, 0, 0)),
                  pl.BlockSpec(memory_space=pl.ANY)],
        out_specs=pl.BlockSpec(memory_space=pl.ANY),
        input_output_aliases={2: 0},
        scratch_shapes=[pltpu.SemaphoreType.DMA],
        compiler_params=_cparams(("arbitrary",)),
        name="moe_dispatch",
    )(dest3, h3, xp_zero)


def _expert_kernel(be_ref, nu_ref, x_ref, wgu_ref, wd_ref, y_ref, wgu_bf, wd_bf, *, f):
    b = pl.program_id(0)

    @pl.when((b == 0) | (be_ref[b] != be_ref[jnp.maximum(b - 1, 0)]))
    def _():
        wgu_bf[...] = wgu_ref[...].astype(BF16)
        wd_bf[...] = wd_ref[...].astype(BF16)

    @pl.when(b < nu_ref[0])
    def _():
        x_lo, x_hi = _unpack_bf16_pairs(x_ref[...])
        half = x_lo.shape[1]
        gate = _dot(x_lo, wgu_bf[0:half, 0:f]) + _dot(x_hi, wgu_bf[half:, 0:f])
        up = _dot(x_lo, wgu_bf[0:half, f:2 * f]) + _dot(x_hi, wgu_bf[half:, f:2 * f])
        act = (gate * jax.nn.sigmoid(gate) * up).astype(BF16)
        y_ref[...] = _dot(act, wd_bf[...])

    @pl.when(b >= nu_ref[0])
    def _():
        y_ref[...] = jnp.zeros(y_ref.shape, F32)


def _experts(block_e, n_used, xp, w_gate_up, w_down, layer):
    n_slots, dh = xp.shape
    d = 2 * dh
    f = w_down.shape[2]
    grid_spec = pltpu.PrefetchScalarGridSpec(
        num_scalar_prefetch=2,
        grid=(n_slots // MOE_ROWS,),
        in_specs=[pl.BlockSpec((MOE_ROWS, dh), lambda b, be, nu: (b, 0)),
                  pl.BlockSpec((None, None, d, 2 * f), lambda b, be, nu: (layer, be[b], 0, 0)),
                  pl.BlockSpec((None, None, f, d), lambda b, be, nu: (layer, be[b], 0, 0))],
        out_specs=pl.BlockSpec((MOE_ROWS, d), lambda b, be, nu: (b, 0)),
        scratch_shapes=[pltpu.VMEM((d, 2 * f), BF16), pltpu.VMEM((f, d), BF16)],
    )
    return pl.pallas_call(
        functools.partial(_expert_kernel, f=f),
        out_shape=jax.ShapeDtypeStruct((n_slots, d), F32),
        grid_spec=grid_spec,
        compiler_params=_cparams(("arbitrary",)),
        name="moe_experts",
    )(block_e, n_used, xp, w_gate_up, w_down)


def _combine_kernel(dest_ref, x_ref, meta_ref, g2_ref, yp_ref, o_ref, ybuf, sem, *, tm):

    def row_copy(g, u, k, slot):
        return pltpu.make_async_copy(yp_ref.at[pl.ds(slot, 1)], ybuf.at[k, g, pl.ds(u, 1)], sem)

    _for_each_row_copy(tm, dest_ref, row_copy, lambda cp: cp.start())
    _for_each_row_copy(tm, dest_ref, row_copy, lambda cp: cp.wait())
    meta = meta_ref[...]
    y = meta[:, :, 4:5] * ybuf[0] + meta[:, :, 5:6] * ybuf[1]
    o_ref[...] = x_ref[...] + g2_ref[...] * y


def _combine(dest3, x1, meta, g2, mod_map, yp, tm):
    rows, d = x1.shape
    gt = tm // SUBLANES
    as_groups = lambda a: a.reshape(a.shape[0] // SUBLANES, SUBLANES, a.shape[1])
    grow = lambda width: pl.BlockSpec((gt, SUBLANES, width), lambda i: (i, 0, 0))
    if g2.shape[1] == 1:
        g2_spec = pl.BlockSpec((None, 1, d), lambda i: (mod_map(i), 0, 0))
    else:
        g2 = g2.reshape(g2.shape[0], g2.shape[1] // SUBLANES, SUBLANES, d)
        g2_spec = pl.BlockSpec((None,) + g2.shape[1:], lambda i: (mod_map(i), 0, 0, 0))
    out = pl.pallas_call(
        functools.partial(_combine_kernel, tm=tm),
        out_shape=jax.ShapeDtypeStruct((rows // SUBLANES, SUBLANES, d), F32),
        grid=(rows // tm,),
        in_specs=[pl.BlockSpec((None, 1, TOP_K * tm), lambda i: (i, 0, 0), memory_space=pltpu.SMEM),
                  grow(d), grow(LANES), g2_spec, pl.BlockSpec(memory_space=pl.ANY)],
        out_specs=grow(d),
        scratch_shapes=[pltpu.VMEM((TOP_K, gt, SUBLANES, d), F32), pltpu.SemaphoreType.DMA],
        compiler_params=_cparams(("arbitrary",)),
        name="moe_combine",
    )(dest3, as_groups(x1), as_groups(meta), g2, yp)
    return out.reshape(rows, d)


def _moe(h2_p, h2_s, x1_p, x1_s, g2_p, g2_s, w_r, b_pad, w_gate_up, w_down, layer, tm_p, seq):
    n_p, d = h2_p.shape
    n_s = h2_s.shape[0]
    n = n_p + n_s
    meta_p, cnt_p, hp_p = _route(h2_p, w_r, b_pad, jnp.zeros((1, LANES), F32), tm_p)
    meta_s, counts, hp_s = _route(h2_s, w_r, b_pad, cnt_p, n_s)
    sizes = counts[0, :N_EXPERTS].astype(jnp.int32)
    padded = ((sizes + MOE_ROWS - 1) // MOE_ROWS) * MOE_ROWS
    pad_end = jnp.cumsum(padded)
    pad_start = pad_end - padded

    def slots(meta):
        e_tok = meta[:, 0:TOP_K].astype(jnp.int32)
        rank = meta[:, TOP_K:2 * TOP_K].astype(jnp.int32)
        start = jnp.sum(jnp.where(e_tok[:, :, None] == jnp.arange(N_EXPERTS)[None, None, :],
                                  pad_start[None, None, :], 0), axis=-1)
        return (start + rank).astype(jnp.int32)

    dest_p = slots(meta_p).reshape(n_p // tm_p, 1, TOP_K * tm_p)
    dest_s = slots(meta_s).reshape(1, 1, TOP_K * n_s)
    n_blocks = (n * TOP_K + MOE_ROWS - 1) // MOE_ROWS + N_EXPERTS
    blk0 = jnp.arange(n_blocks, dtype=jnp.int32) * MOE_ROWS
    block_e = jnp.minimum(jnp.sum(pad_end[None, :] <= blk0[:, None], axis=-1), N_EXPERTS - 1).astype(jnp.int32)
    n_used = (pad_end[-1:] // MOE_ROWS).astype(jnp.int32)

    as_groups = lambda a: a.reshape(a.shape[0] // SUBLANES, SUBLANES, a.shape[1])
    xp = _dispatch(dest_p, as_groups(hp_p), jnp.zeros((n_blocks * MOE_ROWS, d // 2), jnp.uint32), tm_p)
    xp = _dispatch(dest_s, as_groups(hp_s), xp, n_s)
    yp = _experts(block_e, n_used, xp, w_gate_up, w_down, layer)
    per_seq = seq // tm_p
    out_p = _combine(dest_p, x1_p, meta_p, g2_p, lambda i: i // per_seq, yp, tm_p)
    out_s = _combine(dest_s, x1_s, meta_s, g2_s, lambda i: 0, yp, n_s)
    return out_p, out_s


def kernel(x_prompt, x_sample, cache_k, cache_v, state_conv_b, state_conv_c, page_table, c_prompt, c_sample, norm_mix_g, norm_ffn_g, w_ada, b_ada, w_in_a, q_norm_g, k_norm_g, lam_q1, lam_k1, lam_q2, lam_k2, subln_g, convb_w, convb_b, convb_ln_g, convb_ln_b, w_out_a, w_in_c, convc_w, w_out_c, w_group, b_group, w_router, b_router, w_gate_up, w_down):
    bp, seq, d = x_prompt.shape
    bs, t_dec, _ = x_sample.shape
    depth = w_ada.shape[0]
    n_p, n_s = bp * seq, bs * t_dec
    n_pool, page = cache_k.shape[1], cache_k.shape[2]
    n_heads = cache_k.shape[3]
    aw = n_heads * DK_A
    bw = state_conv_b.shape[-1]
    cwid = state_conv_c.shape[-1]
    n_pages = page_table.shape[1]
    past = n_pages * page
    tm_p = _largest_tile(seq, 512, LANES)
    assert n_s % SUBLANES == 0

    x_p = x_prompt.reshape(n_p, d)
    x_s = x_sample.reshape(n_s, d)

    c_all = jnp.concatenate([c_prompt, c_sample], axis=0)
    rows = c_all.shape[0]
    rows_pad = -(-rows // SUBLANES) * SUBLANES
    mods = _ada_all(jnp.pad(c_all, ((0, rows_pad - rows), (0, 0))), w_ada, b_ada)
    mods = mods[:, :rows].reshape(depth, rows, 6, d).transpose(0, 2, 1, 3)
    mods_p = mods[:, :, :bp, None, :]
    mods_s = jnp.repeat(mods[:, :, bp:], t_dec, axis=2)[:, :, None]

    def mod_rows(l, j):
        return mods_p[l, j], mods_s[l, j]

    bd = (jnp.arange(aw)[:, None] // HD_A == jnp.arange(aw)[None, :] // HD_A).astype(BF16)
    cache_kr = cache_k.reshape(-1, DK_A)
    cache_vr = cache_v.reshape(-1, DK_A)
    per_seq = seq // tm_p
    pmap1 = lambda i: i // per_seq
    smap1 = lambda i: 0
    row2 = lambda a: a.reshape(1, -1)

    k_p, v_p, k_s, v_s, cb_p, cb_s, cc_p, cc_s = [], [], [], [], [], [], [], []
    for l in range(depth):
        i = l // 2
        sh1_p, sh1_s = mod_rows(l, 0)
        sc1_p, sc1_s = mod_rows(l, 1)
        g1_p, g1_s = mod_rows(l, 2)
        sh2_p, sh2_s = mod_rows(l, 3)
        sc2_p, sc2_s = mod_rows(l, 4)
        g2_p, g2_s = mod_rows(l, 5)
        gm = row2(norm_mix_g[l])
        gf = row2(norm_ffn_g[l])
        if l % 2 == 0:
            lam_init = 0.8 - 0.6 * math.exp(-0.3 * l)
            w_in = w_in_a[i].astype(BF16)
            w_out = w_out_a[i].astype(BF16)
            qg = jnp.tile(q_norm_g[i].reshape(1, DK_A), (1, n_heads))
            kg = jnp.tile(k_norm_g[i].reshape(1, DK_A), (1, n_heads))
            lamv = jnp.stack([lam_q1[i], lam_k1[i], lam_q2[i], lam_k2[i]])
            sg = row2(subln_g[i])
            cw = jnp.pad(convb_w[i].reshape(CONV_B_WIDTH, bw), ((0, 1), (0, 0)))
            cbb, clg, clb = row2(convb_b[i]), row2(convb_ln_g[i]), row2(convb_ln_b[i])
            halo_pad = ((0, 0), (CONV_B_HALO - (CONV_B_WIDTH - 1), 0), (0, 0))

            qb, kf, kbt, vf, vb, u = _inproj_even(x_p, tm_p, gm, sc1_p, sh1_p, pmap1,
                                                  w_in, bd, qg, kg, aw, bw, True)
            o_p = _attn_prompt(lamv, qb, kbt, vb, sg, bp, seq, lam_init)
            cv_p, nb_p = _convb(u.reshape(bp, seq, bw), jnp.zeros((bp, CONV_B_HALO, bw), F32),
                                cw, cbb, clg, clb, BF16)
            x1_p, h2_p = _outproj_even(o_p, cv_p.reshape(n_p, bw), w_out, x_p, tm_p,
                                       g1_p, gf, sc2_p, sh2_p, pmap1)
            k_p.append(kf.reshape(bp, seq, n_heads, DK_A))
            v_p.append(vf.reshape(bp, seq, n_heads, DK_A))
            cb_p.append(nb_p[:, CONV_B_HALO - (CONV_B_WIDTH - 1):])

            qb, kf, kb, vf, vb, u = _inproj_even(x_s, n_s, gm, sc1_s, sh1_s, smap1,
                                                 w_in, bd, qg, kg, aw, bw, False)
            q4 = qb.reshape(bs, t_dec, n_heads, 2, HD_A).transpose(0, 2, 1, 3, 4)[:, None]
            eye_m = jnp.eye(2, dtype=BF16)[None, :, None, None, :, None]
            q_rows = (q4 * eye_m).reshape(bs, 2 * n_heads * t_dec, DK_A)
            new_pad = ((0, 0), (0, LANES - t_dec * n_heads), (0, 0))
            kn = jnp.pad(kb.reshape(bs, t_dec * n_heads, DK_A), new_pad)
            vn = jnp.pad(vb.reshape(bs, t_dec * n_heads, DK_A), new_pad)
            pt_flat = (page_table.astype(jnp.int32) + i * n_pool).reshape(-1)
            o_s = _attn_sample(pt_flat, lamv, q_rows, cache_kr, cache_vr, kn, vn, sg, n_pages, page,
                               t_dec, past, lam_init)
            o_s = o_s.reshape(bs, n_heads, t_dec, DK_A).transpose(0, 2, 1, 3)
            cv_s, nb_s = _convb(u.reshape(bs, t_dec, bw), jnp.pad(state_conv_b[i], halo_pad),
                                cw, cbb, clg, clb, F32)
            x1_s, h2_s = _outproj_even(o_s.reshape(n_s, aw), cv_s.reshape(n_s, bw), w_out,
                                       x_s, n_s, g1_s, gf, sc2_s, sh2_s, smap1)
            k_s.append(kf.reshape(bs, t_dec, n_heads, DK_A))
            v_s.append(vf.reshape(bs, t_dec, n_heads, DK_A))
            cb_s.append(nb_s[:, CONV_B_HALO - (CONV_B_WIDTH - 1):])
        else:
            w_in = w_in_c[i].astype(BF16)
            w_out = w_out_c[i].astype(BF16)
            cw = jnp.pad(convc_w[i].reshape(CONV_C_WIDTH, cwid), ((0, SUBLANES - CONV_C_WIDTH), (0, 0)))
            halo_p = jnp.zeros((bp, CONV_C_HALO, cwid), F32)
            x1_p, h2_p, nb_p = _odd_mixer(
                x_p, bp, seq, tm_p, gm, sc1_p, sh1_p, lambda b, t: b, w_in, cw, halo_p, w_out,
                g1_p, gf, sc2_p, sh2_p, True, t_dec)
            cc_p.append(nb_p[:, CONV_C_HALO - (CONV_C_WIDTH - 1):])
            st = state_conv_c[i]
            zrow = jnp.zeros((bs, 1, cwid), F32)
            halo1 = jnp.concatenate([st[:, 1:2]] + [zrow] * (t_dec - 1), axis=1).reshape(n_s, cwid)
            halo2 = jnp.concatenate([st[:, 0:1], st[:, 1:2]] + [zrow] * (t_dec - 2), axis=1).reshape(n_s, cwid)
            x1_s, h2_s, z_s = _odd_mixer(
                x_s, 1, n_s, n_s, gm, sc1_s, sh1_s, None, w_in, cw, jnp.stack([halo1, halo2]),
                w_out, g1_s, gf, sc2_s, sh2_s, False, t_dec)
            cc_s.append(z_s.reshape(bs, t_dec, cwid)[:, t_dec - (CONV_C_WIDTH - 1):])

        w_r = jnp.pad(jnp.concatenate([w_group[l], w_router[l]], axis=1),
                      ((0, 0), (0, LANES - N_GROUPS - N_EXPERTS)))
        b_r = jnp.pad(jnp.concatenate([b_group[l], b_router[l]]), (0, LANES - N_GROUPS - N_EXPERTS))
        x_p, x_s = _moe(h2_p, h2_s, x1_p, x1_s, g2_p, g2_s, w_r.astype(BF16), row2(b_r),
                        w_gate_up, w_down, l, tm_p, seq)

    return (x_p.reshape(bp, seq, d), x_s.reshape(bs, t_dec, d),
            jnp.stack(k_p), jnp.stack(v_p), jnp.stack(k_s), jnp.stack(v_s),
            jnp.stack(cb_p), jnp.stack(cb_s), jnp.stack(cc_p), jnp.stack(cc_s))
```
